```python
import math
import jax, jax.numpy as jnp
from jax import lax
import numpy as np

D_MODEL = 1024
BATCH = 4
SEQ = 8192
DEPTH = 2

N_HEADS = 16
HEAD_DIM = D_MODEL // N_HEADS
Q_BLOCK = 128
CONV_WIDTH = 3
D_FF = 2816
N_EXPERTS = 8
TOP_K = 2
RMS_EPS = 1e-6
N_EVEN = (DEPTH + 1) // 2
N_ODD = DEPTH // 2

kernel_name = "fox_shortconv_moe_hybrid"


def rmsnorm(x, g):
    x32 = x.astype(jnp.float32)
    y = x32 * lax.rsqrt(jnp.mean(x32 * x32, axis=-1, keepdims=True) + RMS_EPS)
    return (y * g.astype(jnp.float32)).astype(x.dtype)


def swiglu(t, w_gate, w_up, w_down):
    return (jax.nn.silu(t @ w_gate) * (t @ w_up)) @ w_down


def forgetting_attention(h, w_in, b_forget, w_out):
    bsz, seq, _ = h.shape
    proj = h @ w_in
    q = proj[..., :D_MODEL].reshape(bsz, seq, N_HEADS, HEAD_DIM)
    k = proj[..., D_MODEL:2 * D_MODEL].reshape(bsz, seq, N_HEADS, HEAD_DIM)
    v = proj[..., 2 * D_MODEL:3 * D_MODEL].reshape(bsz, seq, N_HEADS, HEAD_DIM)
    f_logit = proj[..., 3 * D_MODEL:] + b_forget
    log_f = jax.nn.log_sigmoid(f_logit.astype(jnp.float32))
    c = jnp.cumsum(log_f, axis=1).transpose(0, 2, 1)
    scale = 1.0 / math.sqrt(HEAD_DIM)
    key_pos = jnp.arange(seq)

    def query_block(i):
        start = i * Q_BLOCK
        qb = lax.dynamic_slice_in_dim(q, start, Q_BLOCK, axis=1)
        cb = lax.dynamic_slice_in_dim(c, start, Q_BLOCK, axis=2)
        s = jnp.einsum('bqhd,bkhd->bhqk', qb, k).astype(jnp.float32) * scale
        s = s + cb[:, :, :, None] - c[:, :, None, :]
        q_pos = start + jnp.arange(Q_BLOCK)
        causal = key_pos[None, :] <= q_pos[:, None]
        s = jnp.where(causal[None, None], s, -jnp.inf)
        p = jax.nn.softmax(s, axis=-1)
        return jnp.einsum('bhqk,bkhd->bqhd', p.astype(v.dtype), v)

    o = lax.map(query_block, jnp.arange(seq // Q_BLOCK))
    o = o.transpose(1, 0, 2, 3, 4).reshape(bsz, seq, D_MODEL)
    return o @ w_out


def short_conv_mixer(h, w_in, conv_w, w_out):
    proj = h @ w_in
    gate_b = proj[..., :D_MODEL]
    gate_c = proj[..., D_MODEL:2 * D_MODEL]
    xv = proj[..., 2 * D_MODEL:]
    u = gate_c * xv
    conv = lax.conv_general_dilated(
        u, conv_w[:, None, :].astype(u.dtype), window_strides=(1,),
        padding=[(CONV_WIDTH - 1, 0)],
        dimension_numbers=('NWC', 'WIO', 'NWC'),
        feature_group_count=D_MODEL)
    return (gate_b * conv) @ w_out


def expert_swiglu(h, w_router, w_gate, w_up, w_down):
    bsz, seq, _ = h.shape
    t = h.reshape(bsz * seq, D_MODEL)
    logits = (t @ w_router).astype(jnp.float32)
    top_v, top_i = lax.top_k(logits, TOP_K)
    top_w = jax.nn.softmax(top_v, axis=-1)
    gates = jnp.sum(jax.nn.one_hot(top_i, N_EXPERTS, dtype=jnp.float32) * top_w[..., None], axis=1)
    out = jnp.zeros_like(t)
    for e in range(N_EXPERTS):
        y = swiglu(t, w_gate[e], w_up[e], w_down[e])
        out = out + gates[:, e:e + 1].astype(t.dtype) * y
    return out.reshape(bsz, seq, D_MODEL)


def setup_inputs(seed: int = 0) -> dict:
    key = jax.random.key(seed)
    ks = jax.random.split(key, 20)
    D, H, F, E = D_MODEL, N_HEADS, D_FF, N_EXPERTS
    nrm = lambda k, shape, fan_in: jax.random.normal(k, shape, jnp.float32) * fan_in ** -0.5
    gain = lambda k, shape: 1.0 + 0.02 * jax.random.normal(k, shape, jnp.float32)
    return {
        "x": jax.random.normal(ks[0], (BATCH, SEQ, D), jnp.float32),
        "attn_norm": gain(ks[1], (N_EVEN, D)),
        "attn_w_in": nrm(ks[2], (N_EVEN, D, 3 * D + H), D),
        "attn_b_forget": jax.random.uniform(ks[3], (N_EVEN, H), jnp.float32, 1.0, 5.0),
        "attn_w_out": nrm(ks[4], (N_EVEN, D, D), D),
        "ffn_norm": gain(ks[5], (N_EVEN, D)),
        "ffn_w_gate": nrm(ks[6], (N_EVEN, D, F), D),
        "ffn_w_up": nrm(ks[7], (N_EVEN, D, F), D),
        "ffn_w_down": nrm(ks[8], (N_EVEN, F, D), F),
        "conv_norm": gain(ks[9], (N_ODD, D)),
        "conv_w_in": nrm(ks[10], (N_ODD, D, 3 * D), D),
        "conv_w": nrm(ks[11], (N_ODD, CONV_WIDTH, D), CONV_WIDTH),
        "conv_w_out": nrm(ks[12], (N_ODD, D, D), D),
        "moe_norm": gain(ks[13], (N_ODD, D)),
        "moe_w_router": nrm(ks[14], (N_ODD, D, E), D),
        "moe_w_gate": nrm(ks[15], (N_ODD, E, D, F), D),
        "moe_w_up": nrm(ks[16], (N_ODD, E, D, F), D),
        "moe_w_down": nrm(ks[17], (N_ODD, E, F, D), F),
        "final_norm": gain(ks[18], (D,)),
    }


def reference(x, attn_norm, attn_w_in, attn_b_forget, attn_w_out,
              ffn_norm, ffn_w_gate, ffn_w_up, ffn_w_down,
              conv_norm, conv_w_in, conv_w, conv_w_out,
              moe_norm, moe_w_router, moe_w_gate, moe_w_up, moe_w_down,
              final_norm):
    h = x
    for i in range(DEPTH):
        j = i // 2
        if i % 2 == 0:
            h = h + forgetting_attention(rmsnorm(h, attn_norm[j]), attn_w_in[j],
                                         attn_b_forget[j], attn_w_out[j])
            h = h + swiglu(rmsnorm(h, ffn_norm[j]), ffn_w_gate[j], ffn_w_up[j], ffn_w_down[j])
        else:
            h = h + short_conv_mixer(rmsnorm(h, conv_norm[j]), conv_w_in[j],
                                     conv_w[j], conv_w_out[j])
            h = h + expert_swiglu(rmsnorm(h, moe_norm[j]), moe_w_router[j],
                                  moe_w_gate[j], moe_w_up[j], moe_w_down[j])
    return rmsnorm(h, final_norm)
```

```python
import functools
import math

import jax
import jax.numpy as jnp
from jax import lax
from jax.experimental import pallas as pl
from jax.experimental.pallas import tpu as pltpu

RMS_EPS = 1e-6
LANES = 128
NEG_BIG = -1e30
VMEM_LIMIT = 56 * 1024 * 1024

f32 = jnp.float32
bf16 = jnp.bfloat16


def _rmsnorm(x, g):
    return x * lax.rsqrt(jnp.mean(x * x, axis=-1, keepdims=True) + RMS_EPS) * g


def _cumsum_rows(x):
    n = x.shape[0]
    row = lax.broadcasted_iota(jnp.int32, x.shape, 0)
    d = 1
    while d < n:
        x = x + jnp.where(row >= d, pltpu.roll(x, d, 0), 0.0)
        d *= 2
    return x


def _const_spec(shape):
    return pl.BlockSpec(shape, lambda *_: (0,) * len(shape), pipeline_mode=pl.Buffered(1))


def _attn_in_kernel(x_ref, g_ref, wqk_ref, wvt_ref, wf_ref, bf_ref,
                    q_ref, k_ref, vt_ref, c_ref, ct_ref, carry_ref, *, n_heads):
    @pl.when(pl.program_id(1) == 0)
    def _():
        carry_ref[...] = jnp.zeros_like(carry_ref)

    x = x_ref[0]
    d = x.shape[1]
    hn = _rmsnorm(x, g_ref[...]).astype(bf16)
    qk = jnp.dot(hn, wqk_ref[...], preferred_element_type=f32)
    q_ref[0] = qk[:, :d].astype(bf16)
    k_ref[0] = qk[:, d:].astype(bf16)
    vt = lax.dot_general(wvt_ref[...], hn, (((1,), (1,)), ((), ())), preferred_element_type=f32)
    vt_ref[0] = vt.astype(bf16)
    fl = jnp.dot(hn, wf_ref[...], preferred_element_type=f32) + bf_ref[...]
    logf = jnp.minimum(fl, 0.0) - jnp.log1p(jnp.exp(-jnp.abs(fl)))
    c = _cumsum_rows(logf) + carry_ref[...]
    carry_ref[...] = c[c.shape[0] - 1:, :]
    c_ref[0] = c
    ct_ref[0] = c.T[:n_heads, :]


def _attn_in(x, g, wqk, wvt, wf, bfg, *, n_heads, tm):
    b, s, d = x.shape
    grid = (b, s // tm)
    return pl.pallas_call(
        functools.partial(_attn_in_kernel, n_heads=n_heads),
        grid=grid,
        in_specs=[
            pl.BlockSpec((1, tm, d), lambda i, j: (i, j, 0)),
            _const_spec((1, d)),
            _const_spec((d, 2 * d)),
            _const_spec((d, d)),
            _const_spec((d, LANES)),
            _const_spec((1, LANES)),
        ],
        out_specs=[
            pl.BlockSpec((1, tm, d), lambda i, j: (i, j, 0)),
            pl.BlockSpec((1, tm, d), lambda i, j: (i, j, 0)),
            pl.BlockSpec((1, d, tm), lambda i, j: (i, 0, j)),
            pl.BlockSpec((1, tm, LANES), lambda i, j: (i, j, 0)),
            pl.BlockSpec((1, n_heads, tm), lambda i, j: (i, 0, j)),
        ],
        out_shape=[
            jax.ShapeDtypeStruct((b, s, d), bf16),
            jax.ShapeDtypeStruct((b, s, d), bf16),
            jax.ShapeDtypeStruct((b, d, s), bf16),
            jax.ShapeDtypeStruct((b, s, LANES), f32),
            jax.ShapeDtypeStruct((b, n_heads, s), f32),
        ],
        scratch_shapes=[pltpu.VMEM((1, LANES), f32)],
        compiler_params=pltpu.CompilerParams(
            dimension_semantics=("parallel", "arbitrary"), vmem_limit_bytes=VMEM_LIMIT),
        name="attn_in",
    )(x, g, wqk, wvt, wf, bfg)


def _attn_kernel(q_ref, k_ref, vt_ref, c_ref, ct_ref, o_ref, ckb_ref, *, tq, dh, seq):
    hp = pl.program_id(1)
    lane = lax.broadcasted_iota(jnp.int32, (1, LANES), 1)
    rows = 256

    def build(i, carry):
        r0 = pl.multiple_of(i * rows, rows)
        cb = c_ref[0, pl.ds(r0, rows), :]
        for h in range(2):
            col = jnp.sum(jnp.where(lane == 2 * hp + h, cb, 0.0), axis=1, keepdims=True)
            ckb_ref[h, pl.ds(r0, rows), :] = jnp.broadcast_to(col, (rows, LANES))
        return carry

    lax.fori_loop(0, seq // rows, build, 0)

    head_lanes = (lane < dh, lane >= dh)
    n_rep = tq // LANES
    kpos = lax.broadcasted_iota(jnp.int32, (tq, tq), 0)
    qpos = lax.broadcasted_iota(jnp.int32, (tq, tq), 1)

    def q_block(qb, carry0):
        q0 = pl.multiple_of(qb * tq, tq)
        qblk = q_ref[0, pl.ds(q0, tq), :]
        qm = [jnp.where(head_lanes[h], qblk, jnp.zeros_like(qblk)) for h in range(2)]
        cq = [ct_ref[0, 0, h:h + 1, pl.ds(q0, tq)] for h in range(2)]

        def step(j, carry, diagonal):
            k0 = pl.multiple_of(j * tq, tq)
            kb = k_ref[0, pl.ds(k0, tq), :]
            new = []
            for h in range(2):
                m, l, acc = carry[h]
                st = lax.dot_general(kb, qm[h], (((1,), (1,)), ((), ())),
                                     preferred_element_type=f32)
                ck = ckb_ref[h, pl.ds(k0, tq), :]
                t = st - jnp.concatenate([ck] * n_rep, axis=1)
                if diagonal:
                    t = jnp.where(kpos <= qpos, t, NEG_BIG)
                m_new = jnp.maximum(m, jnp.max(t, axis=0, keepdims=True) + cq[h])
                p = jnp.exp(t + (cq[h] - m_new))
                alpha = jnp.exp(m - m_new)
                vt = vt_ref[0, pl.ds(h * dh, dh), pl.ds(k0, tq)]
                pv = jnp.dot(vt, p.astype(bf16), preferred_element_type=f32)
                l = alpha * l + jnp.sum(p, axis=0, keepdims=True)
                acc = alpha * acc + pv
                new.append((m_new, l, acc))
            return tuple(new)

        init = tuple((jnp.full((1, tq), NEG_BIG, f32), jnp.zeros((1, tq), f32),
                      jnp.zeros((dh, tq), f32)) for _ in range(2))
        carry = lax.fori_loop(0, qb, lambda j, c: step(j, c, False), init)
        carry = step(qb, carry, True)
        ot = jnp.concatenate([carry[h][2] / carry[h][1] for h in range(2)], axis=0)
        o_ref[0, pl.ds(q0, tq), :] = ot.T.astype(bf16)
        return carry0

    lax.fori_loop(0, seq // tq, q_block, 0)


def _attention(q, k, vt, c, ct, *, n_heads, tq):
    b, s, d = q.shape
    dh = d // n_heads
    assert 2 * dh == LANES, "kernel packs exactly two heads per 128-lane block"
    n_pairs = n_heads // 2
    ct4 = ct.reshape(b, n_pairs, 2, s)
    return pl.pallas_call(
        functools.partial(_attn_kernel, tq=tq, dh=dh, seq=s),
        grid=(b, n_pairs),
        in_specs=[
            pl.BlockSpec((1, s, LANES), lambda i, j: (i, 0, j)),
            pl.BlockSpec((1, s, LANES), lambda i, j: (i, 0, j)),
            pl.BlockSpec((1, LANES, s), lambda i, j: (i, j, 0)),
            pl.BlockSpec((1, s, LANES), lambda i, j: (i, 0, 0)),
            pl.BlockSpec((1, 1, 2, s), lambda i, j: (i, j, 0, 0)),
        ],
        out_specs=pl.BlockSpec((1, s, LANES), lambda i, j: (i, 0, j)),
        out_shape=jax.ShapeDtypeStruct((b, s, d), bf16),
        scratch_shapes=[pltpu.VMEM((2, s, LANES), f32)],
        compiler_params=pltpu.CompilerParams(
            dimension_semantics=("parallel", "arbitrary"), vmem_limit_bytes=VMEM_LIMIT),
        name="fox_attention",
    )(q, k, vt, c, ct4)


def _ff_chunks(f):
    step = 1024 if f % 256 == 0 and f > 1024 else f
    return [(c0, min(c0 + step, f)) for c0 in range(0, f, step)]


def _swiglu(hn, wg_ref, wu_ref, wd_ref):
    y = None
    for c0, c1 in _ff_chunks(wg_ref.shape[-1]):
        g = jnp.dot(hn, wg_ref[:, c0:c1], preferred_element_type=f32)
        u = jnp.dot(hn, wu_ref[:, c0:c1], preferred_element_type=f32)
        a = (g * (1.0 / (1.0 + jnp.exp(-g))) * u).astype(bf16)
        part = jnp.dot(a, wd_ref[c0:c1, :], preferred_element_type=f32)
        y = part if y is None else y + part
    return y


def _attn_out_ffn_kernel(x_ref, o_ref, wo_ref, g_ref, wg_ref, wu_ref, wd_ref, out_ref):
    h1 = x_ref[...] + jnp.dot(o_ref[...], wo_ref[...], preferred_element_type=f32)
    hn = _rmsnorm(h1, g_ref[...]).astype(bf16)
    out_ref[...] = h1 + _swiglu(hn, wg_ref, wu_ref, wd_ref)


def _attn_out_ffn(x, o, wo, g, wg, wu, wd, *, tm):
    n, d = x.shape
    f = wg.shape[1]
    return pl.pallas_call(
        _attn_out_ffn_kernel,
        grid=(n // tm,),
        in_specs=[
            pl.BlockSpec((tm, d), lambda i: (i, 0)),
            pl.BlockSpec((tm, d), lambda i: (i, 0)),
            _const_spec((d, d)),
            _const_spec((1, d)),
            _const_spec((d, f)),
            _const_spec((d, f)),
            _const_spec((f, d)),
        ],
        out_specs=pl.BlockSpec((tm, d), lambda i: (i, 0)),
        out_shape=jax.ShapeDtypeStruct((n, d), f32),
        compiler_params=pltpu.CompilerParams(
            dimension_semantics=("parallel",), vmem_limit_bytes=VMEM_LIMIT),
        name="attn_out_ffn",
    )(x, o, wo, g, wg, wu, wd)


def _conv_route_kernel(h_ref, g_ref, win_ref, cw_ref, wout_ref, g2_ref, wrh_ref, wrl_ref,
                       h3_ref, hn3_ref, gate_ref, tail_ref, *, n_experts, width):
    @pl.when(pl.program_id(1) == 0)
    def _():
        tail_ref[...] = jnp.zeros_like(tail_ref)

    h = h_ref[0]
    tm, d = h.shape
    hn = _rmsnorm(h, g_ref[...]).astype(bf16)
    proj = jnp.dot(hn, win_ref[...], preferred_element_type=f32)
    gate_b, gate_c, xv = proj[:, :d], proj[:, d:2 * d], proj[:, 2 * d:]
    u = gate_c * xv
    tail = tail_ref[...]
    row8 = lax.broadcasted_iota(jnp.int32, tail.shape, 0)
    conv = cw_ref[width - 1:width, :] * u
    for back in range(1, width):
        shifted = pltpu.roll(u, back, 0)
        head = jnp.where(row8 < back, pltpu.roll(tail, back, 0), shifted[:8])
        shifted = jnp.concatenate([head, shifted[8:]], axis=0)
        conv = conv + cw_ref[width - 1 - back:width - back, :] * shifted
    tail_ref[...] = u[tm - 8:, :]
    y = jnp.dot((gate_b * conv).astype(bf16), wout_ref[...], preferred_element_type=f32)
    h3 = h + y
    h3_ref[0] = h3

    hn3 = _rmsnorm(h3, g2_ref[...])
    hi = hn3.astype(bf16)
    lo = (hn3 - hi.astype(f32)).astype(bf16)
    hn3_ref[0] = hi
    logits = (jnp.dot(hi, wrh_ref[...], preferred_element_type=f32)
              + jnp.dot(lo, wrh_ref[...], preferred_element_type=f32)
              + jnp.dot(hi, wrl_ref[...], preferred_element_type=f32))
    lane = lax.broadcasted_iota(jnp.int32, logits.shape, 1)
    logits = jnp.where(lane < n_experts, logits, -jnp.inf)
    top1 = jnp.max(logits, axis=1, keepdims=True)
    idx1 = jnp.min(jnp.where(logits == top1, lane, LANES), axis=1, keepdims=True)
    rest = jnp.where(lane == idx1, -jnp.inf, logits)
    top2 = jnp.max(rest, axis=1, keepdims=True)
    idx2 = jnp.min(jnp.where(rest == top2, lane, LANES), axis=1, keepdims=True)
    e2 = jnp.exp(top2 - top1)
    w1 = 1.0 / (1.0 + e2)
    w2 = e2 / (1.0 + e2)
    gate_ref[0] = jnp.where(lane == idx1, w1, 0.0) + jnp.where(lane == idx2, w2, 0.0)


def _conv_route(h, g, win, cw, wout, g2, wrh, wrl, *, n_experts, tm):
    b, s, d = h.shape
    width = cw.shape[0]
    return pl.pallas_call(
        functools.partial(_conv_route_kernel, n_experts=n_experts, width=width),
        grid=(b, s // tm),
        in_specs=[
            pl.BlockSpec((1, tm, d), lambda i, j: (i, j, 0)),
            _const_spec((1, d)),
            _const_spec((d, 3 * d)),
            _const_spec((width, d)),
            _const_spec((d, d)),
            _const_spec((1, d)),
            _const_spec((d, LANES)),
            _const_spec((d, LANES)),
        ],
        out_specs=[
            pl.BlockSpec((1, tm, d), lambda i, j: (i, j, 0)),
            pl.BlockSpec((1, tm, d), lambda i, j: (i, j, 0)),
            pl.BlockSpec((1, tm, LANES), lambda i, j: (i, j, 0)),
        ],
        out_shape=[
            jax.ShapeDtypeStruct((b, s, d), f32),
            jax.ShapeDtypeStruct((b, s, d), bf16),
            jax.ShapeDtypeStruct((b, s, LANES), f32),
        ],
        scratch_shapes=[pltpu.VMEM((8, d), f32)],
        compiler_params=pltpu.CompilerParams(
            dimension_semantics=("parallel", "arbitrary"), vmem_limit_bytes=VMEM_LIMIT),
        name="conv_route",
    )(h, g, win, cw, wout, g2, wrh, wrl)


def _moe_dense_kernel(h3_ref, hn_ref, gate_ref, wg_ref, wu_ref, wd_ref, gf_ref, out_ref, acc_ref):
    e = pl.program_id(1)

    @pl.when(e == 0)
    def _():
        acc_ref[...] = h3_ref[...]

    lane = lax.broadcasted_iota(jnp.int32, gate_ref.shape, 1)
    w = jnp.sum(jnp.where(lane == e, gate_ref[...], 0.0), axis=1, keepdims=True)
    acc_ref[...] += w * _swiglu(hn_ref[...], wg_ref.at[0], wu_ref.at[0], wd_ref.at[0])

    @pl.when(e == pl.num_programs(1) - 1)
    def _():
        out_ref[...] = _rmsnorm(acc_ref[...], gf_ref[...])


def _moe_dense(h3, hn3, gate, wg, wu, wd, gf, *, tm):
    n, d = h3.shape
    n_experts, _, f = wg.shape
    return pl.pallas_call(
        _moe_dense_kernel,
        grid=(n // tm, n_experts),
        in_specs=[
            pl.BlockSpec((tm, d), lambda i, e: (i, 0)),
            pl.BlockSpec((tm, d), lambda i, e: (i, 0)),
            pl.BlockSpec((tm, LANES), lambda i, e: (i, 0)),
            pl.BlockSpec((1, d, f), lambda i, e: (e, 0, 0)),
            pl.BlockSpec((1, d, f), lambda i, e: (e, 0, 0)),
            pl.BlockSpec((1, f, d), lambda i, e: (e, 0, 0)),
            _const_spec((1, d)),
        ],
        out_specs=pl.BlockSpec((tm, d), lambda i, e: (i, 0)),
        out_shape=jax.ShapeDtypeStruct((n, d), f32),
        scratch_shapes=[pltpu.VMEM((tm, d), f32)],
        compiler_params=pltpu.CompilerParams(
            dimension_semantics=("parallel", "arbitrary"), vmem_limit_bytes=VMEM_LIMIT),
        name="moe_dense",
    )(h3, hn3, gate, wg, wu, wd, gf)


def _pad_cols(w, width):
    return jnp.pad(w, ((0, 0), (0, width - w.shape[1])))


def kernel(x, attn_norm, attn_w_in, attn_b_forget, attn_w_out, ffn_norm, ffn_w_gate, ffn_w_up,
           ffn_w_down, conv_norm, conv_w_in, conv_w, conv_w_out, moe_norm, moe_w_router,
           moe_w_gate, moe_w_up, moe_w_down, final_norm):
    b, s, d = x.shape
    n_heads = attn_b_forget.shape[-1]
    n_experts = moe_w_router.shape[-1]
    dh = d // n_heads
    n = b * s
    tm = min(512, s)

    w_in = attn_w_in[0]
    scale = 1.0 / math.sqrt(dh)
    wqk = jnp.concatenate([w_in[:, :d] * scale, w_in[:, d:2 * d]], axis=1).astype(bf16)
    wvt = w_in[:, 2 * d:3 * d].T.astype(bf16)
    wf = _pad_cols(w_in[:, 3 * d:], LANES).astype(bf16)
    bfg = _pad_cols(attn_b_forget[0][None, :], LANES)
    q, k, vt, c, ct = _attn_in(x, attn_norm[0][None, :], wqk, wvt, wf, bfg, n_heads=n_heads, tm=tm)
    o = _attention(q, k, vt, c, ct, n_heads=n_heads, tq=min(512, s))
    h2 = _attn_out_ffn(x.reshape(n, d), o.reshape(n, d), attn_w_out[0].astype(bf16),
                       ffn_norm[0][None, :], ffn_w_gate[0].astype(bf16), ffn_w_up[0].astype(bf16),
                       ffn_w_down[0].astype(bf16), tm=tm)

    wr = _pad_cols(moe_w_router[0], LANES)
    wrh = wr.astype(bf16)
    wrl = (wr - wrh.astype(f32)).astype(bf16)
    h3, hn3, gate = _conv_route(h2.reshape(b, s, d), conv_norm[0][None, :], conv_w_in[0].astype(bf16),
                                conv_w[0], conv_w_out[0].astype(bf16), moe_norm[0][None, :], wrh, wrl,
                                n_experts=n_experts, tm=tm)
    out = _moe_dense(h3.reshape(n, d), hn3.reshape(n, d), gate.reshape(n, LANES),
                     moe_w_gate[0].astype(bf16), moe_w_up[0].astype(bf16), moe_w_down[0].astype(bf16),
                     final_norm[None, :], tm=min(512, n))
    return out.reshape(b, s, d)
```

```python
import functools
import math

import jax
import jax.numpy as jnp
from jax import lax
from jax.experimental import pallas as pl
from jax.experimental.pallas import tpu as pltpu

RMS_EPS = 1e-6
LANES = 128
NEG_BIG = -1e30
LOG2E = 1.4426950408889634
BF16_ROWS = 16
SUBLANES = 8
NO_RANK = -float(2 ** 22)
MOE_TOKEN_BLOCK = 256
MOE_GATHER_ROWS = 256
MOE_GATHER_WINDOW = 6
MOE_FFN_ROWS = 512
MOE_COMBINE_BLOCK = 512
VMEM_LIMIT = 56 * 1024 * 1024

f32 = jnp.float32
bf16 = jnp.bfloat16


def _rmsnorm(x, g):
    return x * lax.rsqrt(jnp.mean(x * x, axis=-1, keepdims=True) + RMS_EPS) * g


def _cumsum_rows(x):
    n = x.shape[0]
    row = lax.broadcasted_iota(jnp.int32, x.shape, 0)
    d = 1
    while d < n:
        x = x + jnp.where(row >= d, pltpu.roll(x, d, 0), 0.0)
        d *= 2
    return x


def _const_spec(shape):
    return pl.BlockSpec(shape, lambda *_: (0,) * len(shape), pipeline_mode=pl.Buffered(1))


def _attn_in_kernel(x_ref, g_ref, wqk_ref, wvt_ref, wf_ref, bf_ref,
                    q_ref, k_ref, vt_ref, c_ref, ct_ref, carry_ref, *, n_heads):
    @pl.when(pl.program_id(1) == 0)
    def _():
        carry_ref[...] = jnp.zeros_like(carry_ref)

    x = x_ref[0]
    d = x.shape[1]
    hn = _rmsnorm(x, g_ref[...]).astype(bf16)
    qk = jnp.dot(hn, wqk_ref[...], preferred_element_type=f32)
    q_ref[0] = (qk[:, :d] * (LOG2E / math.sqrt(d // n_heads))).astype(bf16)
    k_ref[0] = qk[:, d:].astype(bf16)
    vt = lax.dot_general(wvt_ref[...], hn, (((1,), (1,)), ((), ())), preferred_element_type=f32)
    vt_ref[0] = vt.astype(bf16)
    fl = jnp.dot(hn, wf_ref[...], preferred_element_type=f32) + bf_ref[...]
    logf = jnp.minimum(fl, 0.0) - jnp.log1p(jnp.exp(-jnp.abs(fl)))
    c = _cumsum_rows(logf * LOG2E) + carry_ref[...]
    carry_ref[...] = c[c.shape[0] - 1:, :]
    c_ref[0] = c
    ct_ref[0] = c.T[:n_heads, :]


def _attn_in(x, g, wqk, wvt, wf, bfg, *, n_heads, tm):
    b, s, d = x.shape
    grid = (b, s // tm)
    return pl.pallas_call(
        functools.partial(_attn_in_kernel, n_heads=n_heads),
        grid=grid,
        in_specs=[
            pl.BlockSpec((1, tm, d), lambda i, j: (i, j, 0)),
            _const_spec((1, d)),
            _const_spec((d, 2 * d)),
            _const_spec((d, d)),
            _const_spec((d, LANES)),
            _const_spec((1, LANES)),
        ],
        out_specs=[
            pl.BlockSpec((1, tm, d), lambda i, j: (i, j, 0)),
            pl.BlockSpec((1, tm, d), lambda i, j: (i, j, 0)),
            pl.BlockSpec((1, d, tm), lambda i, j: (i, 0, j)),
            pl.BlockSpec((1, tm, LANES), lambda i, j: (i, j, 0)),
            pl.BlockSpec((1, n_heads, tm), lambda i, j: (i, 0, j)),
        ],
        out_shape=[
            jax.ShapeDtypeStruct((b, s, d), bf16),
            jax.ShapeDtypeStruct((b, s, d), bf16),
            jax.ShapeDtypeStruct((b, d, s), bf16),
            jax.ShapeDtypeStruct((b, s, LANES), f32),
            jax.ShapeDtypeStruct((b, n_heads, s), f32),
        ],
        scratch_shapes=[pltpu.VMEM((1, LANES), f32)],
        compiler_params=pltpu.CompilerParams(
            dimension_semantics=("parallel", "arbitrary"), vmem_limit_bytes=VMEM_LIMIT),
        name="attn_in",
    )(x, g, wqk, wvt, wf, bfg)


def _attn_kernel(q_ref, k_ref, vt_ref, c_ref, ct_ref, o_ref, kaug_ref, qcat_ref, m_ref, acc_ref,
                 t0_ref, mx0_ref, t1_ref, mx1_ref, p_ref, *, tq, dh, seq):
    hp = pl.program_id(1)
    lane = lax.broadcasted_iota(jnp.int32, (1, LANES), 1)
    rows = 256

    def build(i, carry):
        r0 = pl.multiple_of(i * rows, rows)
        cb = c_ref[0, pl.ds(r0, rows), :]
        aug = jnp.zeros((rows, LANES), f32)
        for h in range(2):
            rest = -jnp.sum(jnp.where(lane == 2 * hp + h, cb, 0.0), axis=1, keepdims=True)
            for piece in range(3):
                part = rest.astype(bf16).astype(f32)
                aug = jnp.where(lane == 3 * h + piece, part, aug)
                rest = rest - part
        kaug_ref[pl.ds(r0, rows), :] = aug.astype(bf16)
        return carry

    lax.fori_loop(0, seq // rows, build, 0)

    head_lanes = (lane < dh, lane >= dh)
    ones_rows = jnp.ones((BF16_ROWS, tq), bf16)
    nt_dims = (((1,), (1,)), ((), ()))
    qw = 2 * LANES
    kc = 32

    def q_block(qb, carry0):
        q0 = pl.multiple_of(qb * tq, tq)
        qblk = q_ref[0, pl.ds(q0, tq), :]
        for h in range(2):
            qm = jnp.where(head_lanes[h], qblk, jnp.zeros_like(qblk))
            pick = jnp.where((lane >= 3 * h) & (lane < 3 * h + 3), 1.0, 0.0).astype(bf16)
            qcat_ref[h] = jnp.concatenate([qm, jnp.broadcast_to(pick, (tq, LANES))], axis=1)
            m_ref[h] = jnp.full((1, tq), NEG_BIG, f32)
            acc_ref[h] = jnp.zeros((dh + BF16_ROWS, tq), f32)

        def scores_into(j, t_ref, mx_ref, diagonal):
            k0 = pl.multiple_of(j * tq, tq)
            kcat = jnp.concatenate([k_ref[0, pl.ds(k0, tq), :], kaug_ref[pl.ds(k0, tq), :]], axis=1)
            for h in range(2):
                for c0 in range(0, tq, qw):
                    st = lax.dot_general(kcat, qcat_ref[h, c0:c0 + qw, :], nt_dims,
                                         preferred_element_type=f32)
                    if diagonal:
                        kpos = lax.broadcasted_iota(jnp.int32, (tq, qw), 0)
                        qpos = lax.broadcasted_iota(jnp.int32, (tq, qw), 1) + c0
                        st = jnp.where(kpos <= qpos, st, NEG_BIG)
                    t_ref[h, :, c0:c0 + qw] = st
                    mx_ref[h, :, c0:c0 + qw] = jnp.max(st, axis=0, keepdims=True)

        def softmax_pv(j, t_ref, mx_ref):
            k0 = pl.multiple_of(j * tq, tq)
            for h in range(2):
                cq = ct_ref[0, 0, h:h + 1, pl.ds(q0, tq)]
                m = m_ref[h]
                m_new = jnp.maximum(m, mx_ref[h] + cq)
                m_ref[h] = m_new
                shift = cq - m_new
                for r0 in range(0, tq, kc):
                    p_ref[h, r0:r0 + kc, :] = jnp.exp2(t_ref[h, r0:r0 + kc, :] + shift).astype(bf16)
                vta = jnp.concatenate([vt_ref[0, pl.ds(h * dh, dh), pl.ds(k0, tq)], ones_rows], axis=0)
                pv = jnp.dot(vta, p_ref[h], preferred_element_type=f32)
                acc_ref[h] = jnp.exp2(m - m_new) * acc_ref[h] + pv

        def block_at(n):
            return jnp.where(n == 0, qb, n - 1)

        scores_into(qb, t0_ref, mx0_ref, True)

        def pair(i, carry):
            n = 2 * i
            scores_into(block_at(n + 1), t1_ref, mx1_ref, False)
            softmax_pv(block_at(n), t0_ref, mx0_ref)
            scores_into(block_at(n + 2), t0_ref, mx0_ref, False)
            softmax_pv(block_at(n + 1), t1_ref, mx1_ref)
            return carry

        lax.fori_loop(0, qb // 2, pair, 0)

        @pl.when(qb % 2 == 1)
        def _():
            scores_into(qb - 1, t1_ref, mx1_ref, False)
            softmax_pv(block_at(qb - 1), t0_ref, mx0_ref)
            softmax_pv(qb - 1, t1_ref, mx1_ref)

        @pl.when(qb % 2 == 0)
        def _():
            softmax_pv(block_at(qb), t0_ref, mx0_ref)

        ot = jnp.concatenate([acc_ref[h, :dh, :] / acc_ref[h, dh:dh + 1, :] for h in range(2)], axis=0)
        o_ref[0, pl.ds(q0, tq), :] = ot.T.astype(bf16)
        return carry0

    lax.fori_loop(0, seq // tq, q_block, 0)


def _attention(q, k, vt, c, ct, *, n_heads, tq):
    b, s, d = q.shape
    dh = d // n_heads
    assert 2 * dh == LANES, "kernel packs exactly two heads per 128-lane block"
    n_pairs = n_heads // 2
    ct4 = ct.reshape(b, n_pairs, 2, s)
    return pl.pallas_call(
        functools.partial(_attn_kernel, tq=tq, dh=dh, seq=s),
        grid=(b, n_pairs),
        in_specs=[
            pl.BlockSpec((1, s, LANES), lambda i, j: (i, 0, j)),
            pl.BlockSpec((1, s, LANES), lambda i, j: (i, 0, j)),
            pl.BlockSpec((1, LANES, s), lambda i, j: (i, j, 0)),
            pl.BlockSpec((1, s, LANES), lambda i, j: (i, 0, 0)),
            pl.BlockSpec((1, 1, 2, s), lambda i, j: (i, j, 0, 0)),
        ],
        out_specs=pl.BlockSpec((1, s, LANES), lambda i, j: (i, 0, j)),
        out_shape=jax.ShapeDtypeStruct((b, s, d), bf16),
        scratch_shapes=[
            pltpu.VMEM((s, LANES), bf16),
            pltpu.VMEM((2, tq, 2 * LANES), bf16),
            pltpu.VMEM((2, 1, tq), f32),
            pltpu.VMEM((2, dh + BF16_ROWS, tq), f32),
            pltpu.VMEM((2, tq, tq), f32), pltpu.VMEM((2, 1, tq), f32),
            pltpu.VMEM((2, tq, tq), f32), pltpu.VMEM((2, 1, tq), f32),
            pltpu.VMEM((2, tq, tq), bf16),
        ],
        compiler_params=pltpu.CompilerParams(
            dimension_semantics=("parallel", "arbitrary"), vmem_limit_bytes=VMEM_LIMIT),
        name="fox_attention",
    )(q, k, vt, c, ct4)


def _ff_chunks(f):
    step = 1024 if f % 256 == 0 and f > 1024 else f
    return [(c0, min(c0 + step, f)) for c0 in range(0, f, step)]


def _swiglu(hn, wg_ref, wu_ref, wd_ref):
    y = None
    for c0, c1 in _ff_chunks(wg_ref.shape[-1]):
        g = jnp.dot(hn, wg_ref[:, c0:c1], preferred_element_type=f32)
        u = jnp.dot(hn, wu_ref[:, c0:c1], preferred_element_type=f32)
        a = (g * (1.0 / (1.0 + jnp.exp(-g))) * u).astype(bf16)
        part = jnp.dot(a, wd_ref[c0:c1, :], preferred_element_type=f32)
        y = part if y is None else y + part
    return y


def _attn_out_ffn_kernel(x_ref, o_ref, wo_ref, g_ref, wg_ref, wu_ref, wd_ref, out_ref):
    h1 = x_ref[...] + jnp.dot(o_ref[...], wo_ref[...], preferred_element_type=f32)
    hn = _rmsnorm(h1, g_ref[...]).astype(bf16)
    out_ref[...] = h1 + _swiglu(hn, wg_ref, wu_ref, wd_ref)


def _attn_out_ffn(x, o, wo, g, wg, wu, wd, *, tm):
    n, d = x.shape
    f = wg.shape[1]
    return pl.pallas_call(
        _attn_out_ffn_kernel,
        grid=(n // tm,),
        in_specs=[
            pl.BlockSpec((tm, d), lambda i: (i, 0)),
            pl.BlockSpec((tm, d), lambda i: (i, 0)),
            _const_spec((d, d)),
            _const_spec((1, d)),
            _const_spec((d, f)),
            _const_spec((d, f)),
            _const_spec((f, d)),
        ],
        out_specs=pl.BlockSpec((tm, d), lambda i: (i, 0)),
        out_shape=jax.ShapeDtypeStruct((n, d), f32),
        compiler_params=pltpu.CompilerParams(
            dimension_semantics=("parallel",), vmem_limit_bytes=VMEM_LIMIT),
        name="attn_out_ffn",
    )(x, o, wo, g, wg, wu, wd)


def _conv_route_kernel(h_ref, g_ref, win_ref, cw_ref, wout_ref, g2_ref, wrh_ref, wrl_ref,
                       h3_ref, hn3_ref, gate_ref, rank_ref, gate_t_ref, rank_t_ref, cend_ref,
                       tail_ref, cnt_ref, *, n_experts, width, tb):
    @pl.when(pl.program_id(1) == 0)
    def _():
        tail_ref[...] = jnp.zeros_like(tail_ref)

    @pl.when((pl.program_id(0) == 0) & (pl.program_id(1) == 0))
    def _():
        cnt_ref[...] = jnp.zeros_like(cnt_ref)

    h = h_ref[0]
    tm, d = h.shape
    hn = _rmsnorm(h, g_ref[...]).astype(bf16)
    proj = jnp.dot(hn, win_ref[...], preferred_element_type=f32)
    gate_b, gate_c, xv = proj[:, :d], proj[:, d:2 * d], proj[:, 2 * d:]
    u = gate_c * xv
    tail = tail_ref[...]
    row8 = lax.broadcasted_iota(jnp.int32, tail.shape, 0)
    conv = cw_ref[width - 1:width, :] * u
    for back in range(1, width):
        shifted = pltpu.roll(u, back, 0)
        head = jnp.where(row8 < back, pltpu.roll(tail, back, 0), shifted[:8])
        shifted = jnp.concatenate([head, shifted[8:]], axis=0)
        conv = conv + cw_ref[width - 1 - back:width - back, :] * shifted
    tail_ref[...] = u[tm - 8:, :]
    y = jnp.dot((gate_b * conv).astype(bf16), wout_ref[...], preferred_element_type=f32)
    h3 = h + y
    h3_ref[0] = h3

    hn3 = _rmsnorm(h3, g2_ref[...])
    hi = hn3.astype(bf16)
    lo = (hn3 - hi.astype(f32)).astype(bf16)
    hn3_ref[0] = hi
    logits = (jnp.dot(hi, wrh_ref[...], preferred_element_type=f32)
              + jnp.dot(lo, wrh_ref[...], preferred_element_type=f32)
              + jnp.dot(hi, wrl_ref[...], preferred_element_type=f32))
    lane = lax.broadcasted_iota(jnp.int32, logits.shape, 1)
    logits = jnp.where(lane < n_experts, logits, -jnp.inf)
    top1 = jnp.max(logits, axis=1, keepdims=True)
    idx1 = jnp.min(jnp.where(logits == top1, lane, LANES), axis=1, keepdims=True)
    rest = jnp.where(lane == idx1, -jnp.inf, logits)
    top2 = jnp.max(rest, axis=1, keepdims=True)
    idx2 = jnp.min(jnp.where(rest == top2, lane, LANES), axis=1, keepdims=True)
    e2 = jnp.exp(top2 - top1)
    w1 = 1.0 / (1.0 + e2)
    w2 = e2 / (1.0 + e2)
    gate = jnp.where(lane == idx1, w1, 0.0) + jnp.where(lane == idx2, w2, 0.0)
    gate_ref[0] = gate
    gate_t_ref[...] = gate.T[:SUBLANES, :]

    sel = jnp.where((lane == idx1) | (lane == idx2), 1.0, 0.0)
    incl = _cumsum_rows(sel) + cnt_ref[...]
    rank = jnp.where(sel > 0.0, incl - 1.0, NO_RANK)
    rank_ref[0] = rank
    rank_t_ref[...] = rank.T[:SUBLANES, :]
    for i in range(tm // tb):
        cend_ref[0, i:i + 1, :] = incl[(i + 1) * tb - 1:(i + 1) * tb, :]
    cnt_ref[...] = incl[tm - 1:, :]


def _conv_route(h, g, win, cw, wout, g2, wrh, wrl, *, n_experts, tm, tb):
    b, s, d = h.shape
    width = cw.shape[0]
    assert n_experts <= SUBLANES
    nb = s // tm
    return pl.pallas_call(
        functools.partial(_conv_route_kernel, n_experts=n_experts, width=width, tb=tb),
        grid=(b, nb),
        in_specs=[
            pl.BlockSpec((1, tm, d), lambda i, j: (i, j, 0)),
            _const_spec((1, d)),
            _const_spec((d, 3 * d)),
            _const_spec((width, d)),
            _const_spec((d, d)),
            _const_spec((1, d)),
            _const_spec((d, LANES)),
            _const_spec((d, LANES)),
        ],
        out_specs=[
            pl.BlockSpec((1, tm, d), lambda i, j: (i, j, 0)),
            pl.BlockSpec((1, tm, d), lambda i, j: (i, j, 0)),
            pl.BlockSpec((1, tm, LANES), lambda i, j: (i, j, 0)),
            pl.BlockSpec((1, tm, LANES), lambda i, j: (i, j, 0)),
            pl.BlockSpec((SUBLANES, tm), lambda i, j: (0, i * nb + j)),
            pl.BlockSpec((SUBLANES, tm), lambda i, j: (0, i * nb + j)),
            pl.BlockSpec((1, tm // tb, LANES), lambda i, j: (i * nb + j, 0, 0)),
        ],
        out_shape=[
            jax.ShapeDtypeStruct((b, s, d), f32),
            jax.ShapeDtypeStruct((b, s, d), bf16),
            jax.ShapeDtypeStruct((b, s, LANES), f32),
            jax.ShapeDtypeStruct((b, s, LANES), f32),
            jax.ShapeDtypeStruct((SUBLANES, b * s), f32),
            jax.ShapeDtypeStruct((SUBLANES, b * s), f32),
            jax.ShapeDtypeStruct((b * nb, tm // tb, LANES), f32),
        ],
        scratch_shapes=[pltpu.VMEM((8, d), f32), pltpu.VMEM((1, LANES), f32)],
        compiler_params=pltpu.CompilerParams(
            dimension_semantics=("arbitrary", "arbitrary"), vmem_limit_bytes=VMEM_LIMIT),
        name="conv_route",
    )(h, g, win, cw, wout, g2, wrh, wrl)


def _moe_gather_kernel(tile_ref, blk_ref, lo_ref, hi_ref, first_ref, last_ref, exp_ref, r0_ref,
                       *refs, n_win, tr, tb):
    hn_refs, rank_refs, gate_refs = refs[:n_win], refs[n_win:2 * n_win], refs[2 * n_win:3 * n_win]
    xs_ref, gs_ref, acc_ref, gacc_ref = refs[3 * n_win:]
    g = pl.program_id(0)

    @pl.when(first_ref[g] == 1)
    def _():
        acc_ref[...] = jnp.zeros_like(acc_ref)
        gacc_ref[...] = jnp.zeros_like(gacc_ref)

    e = exp_ref[g]
    row = lax.broadcasted_iota(jnp.int32, (tr, tb), 0).astype(f32)
    rows_sum, gate_sum = None, None
    for k in range(n_win):
        blk = blk_ref[g] + k
        r0 = jnp.where((blk >= lo_ref[g]) & (blk <= hi_ref[g]), r0_ref[g], 2 ** 20).astype(f32)
        hit = rank_refs[k][pl.ds(e, 1), :] - r0 == row
        onehot = jnp.where(hit, 1.0, 0.0).astype(bf16)
        part = jnp.dot(onehot, hn_refs[k][...], preferred_element_type=f32)
        gpart = jnp.sum(jnp.where(hit, gate_refs[k][pl.ds(e, 1), :], 0.0), axis=1, keepdims=True)
        rows_sum = part if rows_sum is None else rows_sum + part
        gate_sum = gpart if gate_sum is None else gate_sum + gpart
    acc_ref[...] += rows_sum
    gacc_ref[...] += gate_sum

    @pl.when(last_ref[g] == 1)
    def _():
        xs_ref[...] = acc_ref[...].astype(bf16)
        gs_ref[...] = gacc_ref[...]


def _moe_gather(items, hn3, rank_t, gate_t, *, n_rows, n_win, tr, tb):
    n, d = hn3.shape
    nb = n // tb
    n_items = items[0].shape[0]

    def tok_map(k, rows_first):
        def index_map(g, tile, blk, *_):
            b = jnp.minimum(blk[g] + k, nb - 1)
            return (b, 0) if rows_first else (0, b)
        return index_map

    def out_map(g, tile, *_):
        return (tile[g], 0)

    in_specs = ([pl.BlockSpec((tb, d), tok_map(k, True)) for k in range(n_win)]
                + [pl.BlockSpec((SUBLANES, tb), tok_map(k, False)) for k in range(n_win)] * 2)
    return pl.pallas_call(
        functools.partial(_moe_gather_kernel, n_win=n_win, tr=tr, tb=tb),
        grid_spec=pltpu.PrefetchScalarGridSpec(
            num_scalar_prefetch=len(items),
            grid=(n_items,),
            in_specs=in_specs,
            out_specs=[pl.BlockSpec((tr, d), out_map), pl.BlockSpec((tr, LANES), out_map)],
            scratch_shapes=[pltpu.VMEM((tr, d), f32), pltpu.VMEM((tr, LANES), f32)],
        ),
        out_shape=[jax.ShapeDtypeStruct((n_rows, d), bf16), jax.ShapeDtypeStruct((n_rows, LANES), f32)],
        compiler_params=pltpu.CompilerParams(
            dimension_semantics=("arbitrary",), vmem_limit_bytes=VMEM_LIMIT),
        name="moe_gather",
    )(*items, *([hn3] * n_win), *([rank_t] * n_win), *([gate_t] * n_win))


def _moe_ffn_kernel(exp_ref, valid_ref, xs_ref, gs_ref, wg_ref, wu_ref, wd_ref, ys_ref):
    @pl.when(valid_ref[pl.program_id(0)] == 1)
    def _():
        y = _swiglu(xs_ref[...], wg_ref.at[0], wu_ref.at[0], wd_ref.at[0])
        ys_ref[...] = (y * gs_ref[:, 0:1]).astype(bf16)


def _moe_ffn(tile_expert, tile_valid, xs, gs, wg, wu, wd, *, tr):
    n_rows, d = xs.shape
    f = wg.shape[2]
    return pl.pallas_call(
        _moe_ffn_kernel,
        grid_spec=pltpu.PrefetchScalarGridSpec(
            num_scalar_prefetch=2,
            grid=(n_rows // tr,),
            in_specs=[
                pl.BlockSpec((tr, d), lambda i, e, v: (i, 0)),
                pl.BlockSpec((tr, LANES), lambda i, e, v: (i, 0)),
                pl.BlockSpec((1, d, f), lambda i, e, v: (e[i], 0, 0)),
                pl.BlockSpec((1, d, f), lambda i, e, v: (e[i], 0, 0)),
                pl.BlockSpec((1, f, d), lambda i, e, v: (e[i], 0, 0)),
            ],
            out_specs=pl.BlockSpec((tr, d), lambda i, e, v: (i, 0)),
        ),
        out_shape=jax.ShapeDtypeStruct((n_rows, d), bf16),
        compiler_params=pltpu.CompilerParams(
            dimension_semantics=("arbitrary",), vmem_limit_bytes=VMEM_LIMIT),
        name="moe_ffn",
    )(tile_expert, tile_valid, xs, gs, wg, wu, wd)


def _moe_combine_kernel(wb_ref, nb_ref, goff_ref, maxb_ref, h3_ref, rank_ref, *refs,
                        n_experts, n_win, tr):
    ys_refs = refs[:n_experts * n_win]
    gf_ref, out_ref, acc_ref = refs[n_experts * n_win:]
    i = pl.program_id(0)
    acc_ref[...] = h3_ref[...]
    col = lax.broadcasted_iota(jnp.int32, (1, tr), 1).astype(f32)
    for e in range(n_experts):
        row = rank_ref[:, e:e + 1] + goff_ref[e].astype(f32)
        for k in range(n_win):
            @pl.when(k < nb_ref[i * n_experts + e])
            def _(e=e, k=k, row=row):
                base = ((wb_ref[i * n_experts + e] + k) * tr).astype(f32)
                onehot = jnp.where(row - base == col, 1.0, 0.0).astype(bf16)
                acc_ref[...] += jnp.dot(onehot, ys_refs[e * n_win + k][...],
                                        preferred_element_type=f32)
    out_ref[...] = _rmsnorm(acc_ref[...], gf_ref[...])


def _moe_combine(win_blk, win_cnt, goff, maxb, h3, rank, ys, gf, *, n_experts, n_win, tr, tb):
    n, d = h3.shape

    def ys_map(e, k):
        def index_map(i, wb, nb, go, mb):
            return (jnp.minimum(wb[i * n_experts + e] + k, mb[0]), 0)
        return index_map

    tok = lambda i, *_: (i, 0)
    return pl.pallas_call(
        functools.partial(_moe_combine_kernel, n_experts=n_experts, n_win=n_win, tr=tr),
        grid_spec=pltpu.PrefetchScalarGridSpec(
            num_scalar_prefetch=4,
            grid=(n // tb,),
            in_specs=([pl.BlockSpec((tb, d), tok), pl.BlockSpec((tb, LANES), tok)]
                      + [pl.BlockSpec((tr, d), ys_map(e, k))
                         for e in range(n_experts) for k in range(n_win)]
                      + [pl.BlockSpec((1, d), lambda i, *_: (0, 0))]),
            out_specs=pl.BlockSpec((tb, d), tok),
            scratch_shapes=[pltpu.VMEM((tb, d), f32)],
        ),
        out_shape=jax.ShapeDtypeStruct((n, d), f32),
        compiler_params=pltpu.CompilerParams(
            dimension_semantics=("arbitrary",), vmem_limit_bytes=VMEM_LIMIT),
        name="moe_combine",
    )(win_blk, win_cnt, goff, maxb, h3, rank, *([ys] * (n_experts * n_win)), gf)


def _searchsorted_rows(table, values, side):
    return jax.vmap(lambda t, v: jnp.searchsorted(t, v, side=side))(table, values).astype(jnp.int32)


def _moe_plan(cend, *, n_tokens, n_experts, tb, tr_gather, tr_ffn, tb_combine, n_win):
    i32 = jnp.int32
    nb = n_tokens // tb
    counts = cend[-1]
    gsz = (counts + tr_ffn - 1) // tr_ffn * tr_ffn
    gend = jnp.cumsum(gsz).astype(i32)
    goff = gend - gsz
    total = gend[-1]
    n_rows = 2 * n_tokens + n_experts * tr_ffn

    def tiles(tr):
        start = jnp.arange(n_rows // tr, dtype=i32) * tr
        expert = jnp.minimum(jnp.searchsorted(gend, start, side="right").astype(i32), n_experts - 1)
        return start, expert, start < total

    start, expert, valid = tiles(tr_gather)
    r0 = start - goff[expert]
    r1 = jnp.minimum(r0 + tr_gather, counts[expert])
    table = cend.T[expert]
    first_blk = _searchsorted_rows(table, r0, "right")
    last_blk = _searchsorted_rows(table, r1, "left")
    span = jnp.where(valid & (r1 > r0), last_blk - first_blk + 1, 0)
    n_items_tile = jnp.where(valid, jnp.maximum((span + n_win - 1) // n_win, 1), 0)
    item_end = jnp.cumsum(n_items_tile).astype(i32)
    n_items_max = n_rows // tr_gather + n_experts * ((nb + n_win - 1) // n_win)
    g = jnp.arange(n_items_max, dtype=i32)
    live = g < item_end[-1]
    g_eff = jnp.minimum(g, item_end[-1] - 1)
    tile = jnp.searchsorted(item_end, g_eff, side="right").astype(i32)
    k = g_eff - (item_end[tile] - n_items_tile[tile])
    lo = first_blk[tile] + k * n_win
    hi = jnp.where(span[tile] > 0, jnp.minimum(lo + n_win - 1, last_blk[tile]), lo - 1)
    items = (tile, jnp.clip(lo, 0, max(nb - n_win, 0)), lo, jnp.where(live, hi, lo - 1),
             (live & (k == 0)).astype(i32), (live & (k == n_items_tile[tile] - 1)).astype(i32),
             expert[tile], r0[tile])

    _, f_expert, f_valid = tiles(tr_ffn)
    last_valid = jnp.maximum(total // tr_ffn - 1, 0)
    f_expert = jnp.where(f_valid, f_expert, f_expert[last_valid])

    per = tb_combine // tb
    cblk = cend.reshape(n_tokens // tb_combine, per, n_experts)[:, -1, :]
    a = jnp.concatenate([jnp.zeros((1, n_experts), i32), cblk[:-1]], axis=0) + goff[None, :]
    b = cblk + goff[None, :]
    win_blk = a // tr_gather
    win_cnt = jnp.where(b > a, (b - 1) // tr_gather - win_blk + 1, 0)
    maxb = jnp.maximum(total // tr_gather - 1, 0).reshape(1)
    return dict(n_rows=n_rows, items=items, ffn_expert=f_expert, ffn_valid=f_valid.astype(i32),
                win_blk=win_blk.reshape(-1), win_cnt=win_cnt.reshape(-1), goff=goff, maxb=maxb)


def _pad_cols(w, width):
    return jnp.pad(w, ((0, 0), (0, width - w.shape[1])))


def kernel(x, attn_norm, attn_w_in, attn_b_forget, attn_w_out, ffn_norm, ffn_w_gate, ffn_w_up,
           ffn_w_down, conv_norm, conv_w_in, conv_w, conv_w_out, moe_norm, moe_w_router,
           moe_w_gate, moe_w_up, moe_w_down, final_norm):
    b, s, d = x.shape
    n_heads = attn_b_forget.shape[-1]
    n_experts = moe_w_router.shape[-1]
    dh = d // n_heads
    n = b * s
    tm = min(512, s)

    w_in = attn_w_in[0]
    wqk = w_in[:, :2 * d].astype(bf16)
    wvt = w_in[:, 2 * d:3 * d].T.astype(bf16)
    wf = _pad_cols(w_in[:, 3 * d:], LANES).astype(bf16)
    bfg = _pad_cols(attn_b_forget[0][None, :], LANES)
    q, k, vt, c, ct = _attn_in(x, attn_norm[0][None, :], wqk, wvt, wf, bfg, n_heads=n_heads, tm=tm)
    o = _attention(q, k, vt, c, ct, n_heads=n_heads, tq=min(512, s))
    h2 = _attn_out_ffn(x.reshape(n, d), o.reshape(n, d), attn_w_out[0].astype(bf16),
                       ffn_norm[0][None, :], ffn_w_gate[0].astype(bf16), ffn_w_up[0].astype(bf16),
                       ffn_w_down[0].astype(bf16), tm=tm)

    wr = _pad_cols(moe_w_router[0], LANES)
    wrh = wr.astype(bf16)
    wrl = (wr - wrh.astype(f32)).astype(bf16)
    tb = min(MOE_TOKEN_BLOCK, tm)
    h3, hn3, _, rank, gate_t, rank_t, cend = _conv_route(
        h2.reshape(b, s, d), conv_norm[0][None, :], conv_w_in[0].astype(bf16), conv_w[0],
        conv_w_out[0].astype(bf16), moe_norm[0][None, :], wrh, wrl, n_experts=n_experts, tm=tm, tb=tb)
    tb_combine = min(MOE_COMBINE_BLOCK, n)
    plan = _moe_plan(cend.reshape(n // tb, LANES)[:, :n_experts].astype(jnp.int32), n_tokens=n,
                     n_experts=n_experts, tb=tb, tr_gather=MOE_GATHER_ROWS, tr_ffn=MOE_FFN_ROWS,
                     tb_combine=tb_combine, n_win=MOE_GATHER_WINDOW)
    xs, gs = _moe_gather(plan["items"], hn3.reshape(n, d), rank_t, gate_t, n_rows=plan["n_rows"],
                         n_win=MOE_GATHER_WINDOW, tr=MOE_GATHER_ROWS, tb=tb)
    ys = _moe_ffn(plan["ffn_expert"], plan["ffn_valid"], xs, gs, moe_w_gate[0].astype(bf16),
                  moe_w_up[0].astype(bf16), moe_w_down[0].astype(bf16), tr=MOE_FFN_ROWS)
    n_win_combine = (tb_combine - 1 + MOE_GATHER_ROWS - 1) // MOE_GATHER_ROWS + 1
    out = _moe_combine(plan["win_blk"], plan["win_cnt"], plan["goff"], plan["maxb"], h3.reshape(n, d),
                       rank.reshape(n, LANES), ys, final_norm[None, :], n_experts=n_experts,
                       n_win=n_win_combine, tr=MOE_GATHER_ROWS, tb=tb_combine)
    return out.reshape(b, s, d)
```

```python
import functools
import math

import jax
import jax.numpy as jnp
from jax import lax
from jax.experimental import pallas as pl
from jax.experimental.pallas import tpu as pltpu

RMS_EPS = 1e-6
LANES = 128
NEG_BIG = -1e30
LOG2E = 1.4426950408889634
BF16_ROWS = 16
SUBLANES = 8
NO_RANK = -float(2 ** 22)
MOE_TOKEN_BLOCK = 256
MOE_GATHER_ROWS = 256
MOE_GATHER_WINDOW = 6
MOE_FFN_PARTS = 2
MOE_COMBINE_BLOCK = 512
VMEM_LIMIT = 56 * 1024 * 1024

f32 = jnp.float32
bf16 = jnp.bfloat16


def _rmsnorm(x, g):
    return x * lax.rsqrt(jnp.mean(x * x, axis=-1, keepdims=True) + RMS_EPS) * g


def _cumsum_rows(x):
    n = x.shape[0]
    row = lax.broadcasted_iota(jnp.int32, x.shape, 0)
    d = 1
    while d < n:
        x = x + jnp.where(row >= d, pltpu.roll(x, d, 0), 0.0)
        d *= 2
    return x


def _const_spec(shape):
    return pl.BlockSpec(shape, lambda *_: (0,) * len(shape), pipeline_mode=pl.Buffered(1))


def _attn_in_kernel(x_ref, g_ref, wqk_ref, wvt_ref, wf_ref, bf_ref,
                    q_ref, k_ref, vt_ref, c_ref, ct_ref, carry_ref, *, n_heads):
    @pl.when(pl.program_id(1) == 0)
    def _():
        carry_ref[...] = jnp.zeros_like(carry_ref)

    x = x_ref[0]
    d = x.shape[1]
    hn = _rmsnorm(x, g_ref[...]).astype(bf16)
    qk = jnp.dot(hn, wqk_ref[...], preferred_element_type=f32)
    q_ref[0] = (qk[:, :d] * (LOG2E / math.sqrt(d // n_heads))).astype(bf16)
    k_ref[0] = qk[:, d:].astype(bf16)
    vt = lax.dot_general(wvt_ref[...], hn, (((1,), (1,)), ((), ())), preferred_element_type=f32)
    vt_ref[0] = vt.astype(bf16)
    fl = jnp.dot(hn, wf_ref[...], preferred_element_type=f32) + bf_ref[...]
    logf = jnp.minimum(fl, 0.0) - jnp.log1p(jnp.exp(-jnp.abs(fl)))
    c = _cumsum_rows(logf * LOG2E) + carry_ref[...]
    carry_ref[...] = c[c.shape[0] - 1:, :]
    c_ref[0] = c
    ct_ref[0] = c.T[:n_heads, :]


def _attn_in(x, g, wqk, wvt, wf, bfg, *, n_heads, tm):
    b, s, d = x.shape
    grid = (b, s // tm)
    return pl.pallas_call(
        functools.partial(_attn_in_kernel, n_heads=n_heads),
        grid=grid,
        in_specs=[
            pl.BlockSpec((1, tm, d), lambda i, j: (i, j, 0)),
            _const_spec((1, d)),
            _const_spec((d, 2 * d)),
            _const_spec((d, d)),
            _const_spec((d, LANES)),
            _const_spec((1, LANES)),
        ],
        out_specs=[
            pl.BlockSpec((1, tm, d), lambda i, j: (i, j, 0)),
            pl.BlockSpec((1, tm, d), lambda i, j: (i, j, 0)),
            pl.BlockSpec((1, d, tm), lambda i, j: (i, 0, j)),
            pl.BlockSpec((1, tm, LANES), lambda i, j: (i, j, 0)),
            pl.BlockSpec((1, n_heads, tm), lambda i, j: (i, 0, j)),
        ],
        out_shape=[
            jax.ShapeDtypeStruct((b, s, d), bf16),
            jax.ShapeDtypeStruct((b, s, d), bf16),
            jax.ShapeDtypeStruct((b, d, s), bf16),
            jax.ShapeDtypeStruct((b, s, LANES), f32),
            jax.ShapeDtypeStruct((b, n_heads, s), f32),
        ],
        scratch_shapes=[pltpu.VMEM((1, LANES), f32)],
        compiler_params=pltpu.CompilerParams(
            dimension_semantics=("parallel", "arbitrary"), vmem_limit_bytes=VMEM_LIMIT),
        name="attn_in",
    )(x, g, wqk, wvt, wf, bfg)


def _attn_kernel(q_ref, k_ref, vt_ref, c_ref, ct_ref, o_ref, kaug_ref, qcat_ref, m_ref, acc_ref,
                 t0_ref, mx0_ref, t1_ref, mx1_ref, p_ref, *, tq, dh, seq):
    hp = pl.program_id(1)
    lane = lax.broadcasted_iota(jnp.int32, (1, LANES), 1)
    rows = 256

    def build(i, carry):
        r0 = pl.multiple_of(i * rows, rows)
        cb = c_ref[0, pl.ds(r0, rows), :]
        aug = jnp.zeros((rows, LANES), f32)
        for h in range(2):
            rest = -jnp.sum(jnp.where(lane == 2 * hp + h, cb, 0.0), axis=1, keepdims=True)
            for piece in range(3):
                part = rest.astype(bf16).astype(f32)
                aug = jnp.where(lane == 3 * h + piece, part, aug)
                rest = rest - part
        kaug_ref[pl.ds(r0, rows), :] = aug.astype(bf16)
        return carry

    lax.fori_loop(0, seq // rows, build, 0)

    head_lanes = (lane < dh, lane >= dh)
    ones_rows = jnp.ones((BF16_ROWS, tq), bf16)
    nt_dims = (((1,), (1,)), ((), ()))
    qw = 2 * LANES
    kc = 32

    def q_block(qb, carry0):
        q0 = pl.multiple_of(qb * tq, tq)
        qblk = q_ref[0, pl.ds(q0, tq), :]
        for h in range(2):
            qm = jnp.where(head_lanes[h], qblk, jnp.zeros_like(qblk))
            pick = jnp.where((lane >= 3 * h) & (lane < 3 * h + 3), 1.0, 0.0).astype(bf16)
            qcat_ref[h] = jnp.concatenate([qm, jnp.broadcast_to(pick, (tq, LANES))], axis=1)
            m_ref[h] = jnp.full((1, tq), NEG_BIG, f32)
            acc_ref[h] = jnp.zeros((dh + BF16_ROWS, tq), f32)

        def scores_into(j, t_ref, mx_ref, diagonal):
            k0 = pl.multiple_of(j * tq, tq)
            kcat = jnp.concatenate([k_ref[0, pl.ds(k0, tq), :], kaug_ref[pl.ds(k0, tq), :]], axis=1)
            for h in range(2):
                for c0 in range(0, tq, qw):
                    st = lax.dot_general(kcat, qcat_ref[h, c0:c0 + qw, :], nt_dims,
                                         preferred_element_type=f32)
                    if diagonal:
                        kpos = lax.broadcasted_iota(jnp.int32, (tq, qw), 0)
                        qpos = lax.broadcasted_iota(jnp.int32, (tq, qw), 1) + c0
                        st = jnp.where(kpos <= qpos, st, NEG_BIG)
                    t_ref[h, :, c0:c0 + qw] = st
                    mx_ref[h, :, c0:c0 + qw] = jnp.max(st, axis=0, keepdims=True)

        def softmax_pv(j, t_ref, mx_ref):
            k0 = pl.multiple_of(j * tq, tq)
            for h in range(2):
                cq = ct_ref[0, 0, h:h + 1, pl.ds(q0, tq)]
                m = m_ref[h]
                m_new = jnp.maximum(m, mx_ref[h] + cq)
                m_ref[h] = m_new
                shift = cq - m_new
                for r0 in range(0, tq, kc):
                    p_ref[h, r0:r0 + kc, :] = jnp.exp2(t_ref[h, r0:r0 + kc, :] + shift).astype(bf16)
                vta = jnp.concatenate([vt_ref[0, pl.ds(h * dh, dh), pl.ds(k0, tq)], ones_rows], axis=0)
                pv = jnp.dot(vta, p_ref[h], preferred_element_type=f32)
                acc_ref[h] = jnp.exp2(m - m_new) * acc_ref[h] + pv

        def block_at(n):
            return jnp.where(n == 0, qb, n - 1)

        scores_into(qb, t0_ref, mx0_ref, True)

        def pair(i, carry):
            n = 2 * i
            scores_into(block_at(n + 1), t1_ref, mx1_ref, False)
            softmax_pv(block_at(n), t0_ref, mx0_ref)
            scores_into(block_at(n + 2), t0_ref, mx0_ref, False)
            softmax_pv(block_at(n + 1), t1_ref, mx1_ref)
            return carry

        lax.fori_loop(0, qb // 2, pair, 0)

        @pl.when(qb % 2 == 1)
        def _():
            scores_into(qb - 1, t1_ref, mx1_ref, False)
            softmax_pv(block_at(qb - 1), t0_ref, mx0_ref)
            softmax_pv(qb - 1, t1_ref, mx1_ref)

        @pl.when(qb % 2 == 0)
        def _():
            softmax_pv(block_at(qb), t0_ref, mx0_ref)

        ot = jnp.concatenate([acc_ref[h, :dh, :] / acc_ref[h, dh:dh + 1, :] for h in range(2)], axis=0)
        o_ref[0, pl.ds(q0, tq), :] = ot.T.astype(bf16)
        return carry0

    lax.fori_loop(0, seq // tq, q_block, 0)


def _attention(q, k, vt, c, ct, *, n_heads, tq):
    b, s, d = q.shape
    dh = d // n_heads
    assert 2 * dh == LANES, "kernel packs exactly two heads per 128-lane block"
    n_pairs = n_heads // 2
    ct4 = ct.reshape(b, n_pairs, 2, s)
    return pl.pallas_call(
        functools.partial(_attn_kernel, tq=tq, dh=dh, seq=s),
        grid=(b, n_pairs),
        in_specs=[
            pl.BlockSpec((1, s, LANES), lambda i, j: (i, 0, j)),
            pl.BlockSpec((1, s, LANES), lambda i, j: (i, 0, j)),
            pl.BlockSpec((1, LANES, s), lambda i, j: (i, j, 0)),
            pl.BlockSpec((1, s, LANES), lambda i, j: (i, 0, 0)),
            pl.BlockSpec((1, 1, 2, s), lambda i, j: (i, j, 0, 0)),
        ],
        out_specs=pl.BlockSpec((1, s, LANES), lambda i, j: (i, 0, j)),
        out_shape=jax.ShapeDtypeStruct((b, s, d), bf16),
        scratch_shapes=[
            pltpu.VMEM((s, LANES), bf16),
            pltpu.VMEM((2, tq, 2 * LANES), bf16),
            pltpu.VMEM((2, 1, tq), f32),
            pltpu.VMEM((2, dh + BF16_ROWS, tq), f32),
            pltpu.VMEM((2, tq, tq), f32), pltpu.VMEM((2, 1, tq), f32),
            pltpu.VMEM((2, tq, tq), f32), pltpu.VMEM((2, 1, tq), f32),
            pltpu.VMEM((2, tq, tq), bf16),
        ],
        compiler_params=pltpu.CompilerParams(
            dimension_semantics=("parallel", "arbitrary"), vmem_limit_bytes=VMEM_LIMIT),
        name="fox_attention",
    )(q, k, vt, c, ct4)


def _ff_chunks(f):
    step = 1024 if f % 256 == 0 and f > 1024 else f
    return [(c0, min(c0 + step, f)) for c0 in range(0, f, step)]


def _swiglu(hn, wg_ref, wu_ref, wd_ref):
    y = None
    for c0, c1 in _ff_chunks(wg_ref.shape[-1]):
        g = jnp.dot(hn, wg_ref[:, c0:c1], preferred_element_type=f32)
        u = jnp.dot(hn, wu_ref[:, c0:c1], preferred_element_type=f32)
        a = (g * (1.0 / (1.0 + jnp.exp(-g))) * u).astype(bf16)
        part = jnp.dot(a, wd_ref[c0:c1, :], preferred_element_type=f32)
        y = part if y is None else y + part
    return y


def _attn_out_ffn_kernel(x_ref, o_ref, wo_ref, g_ref, wg_ref, wu_ref, wd_ref, out_ref):
    h1 = x_ref[...] + jnp.dot(o_ref[...], wo_ref[...], preferred_element_type=f32)
    hn = _rmsnorm(h1, g_ref[...]).astype(bf16)
    out_ref[...] = h1 + _swiglu(hn, wg_ref, wu_ref, wd_ref)


def _attn_out_ffn(x, o, wo, g, wg, wu, wd, *, tm):
    n, d = x.shape
    f = wg.shape[1]
    return pl.pallas_call(
        _attn_out_ffn_kernel,
        grid=(n // tm,),
        in_specs=[
            pl.BlockSpec((tm, d), lambda i: (i, 0)),
            pl.BlockSpec((tm, d), lambda i: (i, 0)),
            _const_spec((d, d)),
            _const_spec((1, d)),
            _const_spec((d, f)),
            _const_spec((d, f)),
            _const_spec((f, d)),
        ],
        out_specs=pl.BlockSpec((tm, d), lambda i: (i, 0)),
        out_shape=jax.ShapeDtypeStruct((n, d), f32),
        compiler_params=pltpu.CompilerParams(
            dimension_semantics=("parallel",), vmem_limit_bytes=VMEM_LIMIT),
        name="attn_out_ffn",
    )(x, o, wo, g, wg, wu, wd)


def _conv_route_kernel(h_ref, g_ref, win_ref, cw_ref, wout_ref, g2_ref, wrh_ref, wrl_ref,
                       h3_ref, hn3_ref, gate_ref, rank_ref, gate_t_ref, rank_t_ref, cend_ref,
                       tail_ref, cnt_ref, *, n_experts, width, tb):
    @pl.when(pl.program_id(1) == 0)
    def _():
        tail_ref[...] = jnp.zeros_like(tail_ref)

    @pl.when((pl.program_id(0) == 0) & (pl.program_id(1) == 0))
    def _():
        cnt_ref[...] = jnp.zeros_like(cnt_ref)

    h = h_ref[0]
    tm, d = h.shape
    hn = _rmsnorm(h, g_ref[...]).astype(bf16)
    proj = jnp.dot(hn, win_ref[...], preferred_element_type=f32)
    gate_b, gate_c, xv = proj[:, :d], proj[:, d:2 * d], proj[:, 2 * d:]
    u = gate_c * xv
    tail = tail_ref[...]
    row8 = lax.broadcasted_iota(jnp.int32, tail.shape, 0)
    conv = cw_ref[width - 1:width, :] * u
    for back in range(1, width):
        shifted = pltpu.roll(u, back, 0)
        head = jnp.where(row8 < back, pltpu.roll(tail, back, 0), shifted[:8])
        shifted = jnp.concatenate([head, shifted[8:]], axis=0)
        conv = conv + cw_ref[width - 1 - back:width - back, :] * shifted
    tail_ref[...] = u[tm - 8:, :]
    y = jnp.dot((gate_b * conv).astype(bf16), wout_ref[...], preferred_element_type=f32)
    h3 = h + y
    h3_ref[0] = h3

    hn3 = _rmsnorm(h3, g2_ref[...])
    hi = hn3.astype(bf16)
    lo = (hn3 - hi.astype(f32)).astype(bf16)
    hn3_ref[0] = hi
    logits = (jnp.dot(hi, wrh_ref[...], preferred_element_type=f32)
              + jnp.dot(lo, wrh_ref[...], preferred_element_type=f32)
              + jnp.dot(hi, wrl_ref[...], preferred_element_type=f32))
    lane = lax.broadcasted_iota(jnp.int32, logits.shape, 1)
    logits = jnp.where(lane < n_experts, logits, -jnp.inf)
    top1 = jnp.max(logits, axis=1, keepdims=True)
    idx1 = jnp.min(jnp.where(logits == top1, lane, LANES), axis=1, keepdims=True)
    rest = jnp.where(lane == idx1, -jnp.inf, logits)
    top2 = jnp.max(rest, axis=1, keepdims=True)
    idx2 = jnp.min(jnp.where(rest == top2, lane, LANES), axis=1, keepdims=True)
    e2 = jnp.exp(top2 - top1)
    w1 = 1.0 / (1.0 + e2)
    w2 = e2 / (1.0 + e2)
    gate = jnp.where(lane == idx1, w1, 0.0) + jnp.where(lane == idx2, w2, 0.0)
    gate_ref[0] = gate
    gate_t_ref[...] = gate.T[:SUBLANES, :]

    sel = jnp.where((lane == idx1) | (lane == idx2), 1.0, 0.0)
    incl = _cumsum_rows(sel) + cnt_ref[...]
    rank = jnp.where(sel > 0.0, incl - 1.0, NO_RANK)
    rank_ref[0] = rank
    rank_t_ref[...] = rank.T[:SUBLANES, :]
    for i in range(tm // tb):
        cend_ref[0, i:i + 1, :] = incl[(i + 1) * tb - 1:(i + 1) * tb, :]
    cnt_ref[...] = incl[tm - 1:, :]


def _conv_route(h, g, win, cw, wout, g2, wrh, wrl, *, n_experts, tm, tb):
    b, s, d = h.shape
    width = cw.shape[0]
    assert n_experts <= SUBLANES
    nb = s // tm
    return pl.pallas_call(
        functools.partial(_conv_route_kernel, n_experts=n_experts, width=width, tb=tb),
        grid=(b, nb),
        in_specs=[
            pl.BlockSpec((1, tm, d), lambda i, j: (i, j, 0)),
            _const_spec((1, d)),
            _const_spec((d, 3 * d)),
            _const_spec((width, d)),
            _const_spec((d, d)),
            _const_spec((1, d)),
            _const_spec((d, LANES)),
            _const_spec((d, LANES)),
        ],
        out_specs=[
            pl.BlockSpec((1, tm, d), lambda i, j: (i, j, 0)),
            pl.BlockSpec((1, tm, d), lambda i, j: (i, j, 0)),
            pl.BlockSpec((1, tm, LANES), lambda i, j: (i, j, 0)),
            pl.BlockSpec((1, tm, LANES), lambda i, j: (i, j, 0)),
            pl.BlockSpec((SUBLANES, tm), lambda i, j: (0, i * nb + j)),
            pl.BlockSpec((SUBLANES, tm), lambda i, j: (0, i * nb + j)),
            pl.BlockSpec((1, tm // tb, LANES), lambda i, j: (i * nb + j, 0, 0)),
        ],
        out_shape=[
            jax.ShapeDtypeStruct((b, s, d), f32),
            jax.ShapeDtypeStruct((b, s, d), bf16),
            jax.ShapeDtypeStruct((b, s, LANES), f32),
            jax.ShapeDtypeStruct((b, s, LANES), f32),
            jax.ShapeDtypeStruct((SUBLANES, b * s), f32),
            jax.ShapeDtypeStruct((SUBLANES, b * s), f32),
            jax.ShapeDtypeStruct((b * nb, tm // tb, LANES), f32),
        ],
        scratch_shapes=[pltpu.VMEM((8, d), f32), pltpu.VMEM((1, LANES), f32)],
        compiler_params=pltpu.CompilerParams(
            dimension_semantics=("arbitrary", "arbitrary"), vmem_limit_bytes=VMEM_LIMIT),
        name="conv_route",
    )(h, g, win, cw, wout, g2, wrh, wrl)


N_ITEM_FIELDS = 5
N_PART_FIELDS = 4


def _moe_expert_kernel(*refs, n_parts, n_win, tg, tb):
    n_scalar = N_ITEM_FIELDS + N_PART_FIELDS * n_parts
    tile_ref, live_ref, first_ref, last_ref, exp_ref = refs[:N_ITEM_FIELDS]
    part_refs = refs[N_ITEM_FIELDS:n_scalar]
    n_blk = n_parts * n_win
    hn_refs = refs[n_scalar:n_scalar + n_blk]
    rank_refs = refs[n_scalar + n_blk:n_scalar + 2 * n_blk]
    gate_refs = refs[n_scalar + 2 * n_blk:n_scalar + 3 * n_blk]
    wg_ref, wu_ref, wd_ref, ys_ref, acc_ref, gacc_ref = refs[n_scalar + 3 * n_blk:]
    g = pl.program_id(0)

    @pl.when(first_ref[g] == 1)
    def _():
        acc_ref[...] = jnp.zeros_like(acc_ref)
        gacc_ref[...] = jnp.zeros_like(gacc_ref)

    @pl.when(live_ref[g] == 1)
    def _():
        e = exp_ref[g]
        row = lax.broadcasted_iota(jnp.int32, (tg, tb), 0).astype(f32)
        for part in range(n_parts):
            blk_ref, lo_ref, hi_ref, r0_ref = part_refs[N_PART_FIELDS * part:N_PART_FIELDS * (part + 1)]
            rows_sum, gate_sum = None, None
            for k in range(n_win):
                blk = blk_ref[g] + k
                r0 = jnp.where((blk >= lo_ref[g]) & (blk <= hi_ref[g]), r0_ref[g], 2 ** 20).astype(f32)
                src = part * n_win + k
                hit = rank_refs[src][pl.ds(e, 1), :] - r0 == row
                onehot = jnp.where(hit, 1.0, 0.0).astype(bf16)
                rows = jnp.dot(onehot, hn_refs[src][...], preferred_element_type=f32)
                gates = jnp.sum(jnp.where(hit, gate_refs[src][pl.ds(e, 1), :], 0.0), axis=1, keepdims=True)
                rows_sum = rows if rows_sum is None else rows_sum + rows
                gate_sum = gates if gate_sum is None else gate_sum + gates
            acc_ref[part * tg:(part + 1) * tg, :] += rows_sum
            gacc_ref[part * tg:(part + 1) * tg, :] += gate_sum

    @pl.when(last_ref[g] == 1)
    def _():
        y = _swiglu(acc_ref[...].astype(bf16), wg_ref.at[0], wu_ref.at[0], wd_ref.at[0])
        ys_ref[...] = (y * gacc_ref[:, 0:1]).astype(bf16)


def _moe_expert(items, hn3, rank_t, gate_t, wg, wu, wd, *, n_rows, n_parts, n_win, tg, tb):
    n, d = hn3.shape
    f = wg.shape[2]
    nb = n // tb
    tr = n_parts * tg
    n_items = items[0].shape[0]

    def tok_map(part, k, rows_first):
        def index_map(g, *scalars):
            blk = scalars[N_ITEM_FIELDS + N_PART_FIELDS * part]
            b = jnp.minimum(blk[g] + k, nb - 1)
            return (b, 0) if rows_first else (0, b)
        return index_map

    def weight_spec(shape):
        return pl.BlockSpec(shape, lambda g, *scalars: (scalars[4][g], 0, 0), pipeline_mode=pl.Buffered(1))

    windows = [(part, k) for part in range(n_parts) for k in range(n_win)]
    in_specs = ([pl.BlockSpec((tb, d), tok_map(part, k, True)) for part, k in windows]
                + [pl.BlockSpec((SUBLANES, tb), tok_map(part, k, False)) for part, k in windows]
                + [pl.BlockSpec((SUBLANES, tb), tok_map(part, k, False)) for part, k in windows]
                + [weight_spec((1, d, f)), weight_spec((1, d, f)), weight_spec((1, f, d))])
    n_blk = len(windows)
    return pl.pallas_call(
        functools.partial(_moe_expert_kernel, n_parts=n_parts, n_win=n_win, tg=tg, tb=tb),
        grid_spec=pltpu.PrefetchScalarGridSpec(
            num_scalar_prefetch=len(items),
            grid=(n_items,),
            in_specs=in_specs,
            out_specs=pl.BlockSpec((tr, d), lambda g, *scalars: (scalars[0][g], 0)),
            scratch_shapes=[pltpu.VMEM((tr, d), f32), pltpu.VMEM((tr, LANES), f32)],
        ),
        out_shape=jax.ShapeDtypeStruct((n_rows, d), bf16),
        compiler_params=pltpu.CompilerParams(
            dimension_semantics=("arbitrary",), vmem_limit_bytes=VMEM_LIMIT),
        name="moe_expert",
    )(*items, *([hn3] * n_blk), *([rank_t] * n_blk), *([gate_t] * n_blk), wg, wu, wd)


def _moe_combine_kernel(wb_ref, nb_ref, goff_ref, maxb_ref, h3_ref, rank_ref, *refs,
                        n_experts, n_win, tr):
    ys_refs = refs[:n_experts * n_win]
    gf_ref, out_ref, acc_ref = refs[n_experts * n_win:]
    i = pl.program_id(0)
    col = lax.broadcasted_iota(jnp.int32, (1, tr), 1).astype(f32)

    def picked(e, k):
        row = rank_ref[:, e:e + 1] + goff_ref[e].astype(f32)
        base = ((wb_ref[i * n_experts + e] + k) * tr).astype(f32)
        onehot = jnp.where(row - base == col, 1.0, 0.0).astype(bf16)
        return jnp.dot(onehot, ys_refs[e * n_win + k][...], preferred_element_type=f32)

    total = h3_ref[...]
    for e in range(n_experts):
        total = total + picked(e, 0)
    acc_ref[...] = total
    for e in range(n_experts):
        for k in range(1, n_win):
            @pl.when(k < nb_ref[i * n_experts + e])
            def _(e=e, k=k):
                acc_ref[...] += picked(e, k)
    out_ref[...] = _rmsnorm(acc_ref[...], gf_ref[...])


def _moe_combine(win_blk, win_cnt, goff, maxb, h3, rank, ys, gf, *, n_experts, n_win, tr, tb):
    n, d = h3.shape

    def ys_map(e, k):
        def index_map(i, wb, nb, go, mb):
            return (jnp.minimum(wb[i * n_experts + e] + k, mb[0]), 0)
        return index_map

    tok = lambda i, *_: (i, 0)
    return pl.pallas_call(
        functools.partial(_moe_combine_kernel, n_experts=n_experts, n_win=n_win, tr=tr),
        grid_spec=pltpu.PrefetchScalarGridSpec(
            num_scalar_prefetch=4,
            grid=(n // tb,),
            in_specs=([pl.BlockSpec((tb, d), tok), pl.BlockSpec((tb, LANES), tok)]
                      + [pl.BlockSpec((tr, d), ys_map(e, k))
                         for e in range(n_experts) for k in range(n_win)]
                      + [pl.BlockSpec((1, d), lambda i, *_: (0, 0))]),
            out_specs=pl.BlockSpec((tb, d), tok),
            scratch_shapes=[pltpu.VMEM((tb, d), f32)],
        ),
        out_shape=jax.ShapeDtypeStruct((n, d), f32),
        compiler_params=pltpu.CompilerParams(
            dimension_semantics=("arbitrary",), vmem_limit_bytes=VMEM_LIMIT),
        name="moe_combine",
    )(win_blk, win_cnt, goff, maxb, h3, rank, *([ys] * (n_experts * n_win)), gf)


def _count_below(table, values, inclusive):
    cmp = table <= values[:, None] if inclusive else table < values[:, None]
    return jnp.sum(cmp, axis=1).astype(jnp.int32)


def _moe_plan(cend, *, n_tokens, n_experts, tb, tr_gather, n_parts, tb_combine, n_win):
    i32 = jnp.int32
    nb = n_tokens // tb
    tr_ffn = n_parts * tr_gather
    counts = cend[-1]
    gsz = (counts + tr_ffn - 1) // tr_ffn * tr_ffn
    gend = jnp.cumsum(gsz).astype(i32)
    goff = gend - gsz
    total = gend[-1]
    n_rows = 2 * n_tokens + n_experts * tr_ffn

    start = jnp.arange(n_rows // tr_gather, dtype=i32) * tr_gather
    expert = jnp.minimum(_count_below(gend[None, :], start, True), n_experts - 1)
    r0 = start - goff[expert]
    r1 = jnp.minimum(r0 + tr_gather, counts[expert])
    table = cend.T[expert]
    first_blk = _count_below(table, r0, True)
    last_blk = _count_below(table, r1, False)
    span = jnp.where((start < total) & (r1 > r0), last_blk - first_blk + 1, 0)

    n_tiles = n_rows // tr_ffn
    tile_valid = jnp.arange(n_tiles, dtype=i32) * tr_ffn < total
    need = jnp.max(((span + n_win - 1) // n_win).reshape(n_tiles, n_parts), axis=1)
    n_items_tile = jnp.where(tile_valid, jnp.maximum(need, 1), 0)
    item_end = jnp.cumsum(n_items_tile).astype(i32)
    n_items_max = n_tiles + n_experts * ((nb + n_win - 1) // n_win)
    g = jnp.arange(n_items_max, dtype=i32)
    live = g < item_end[-1]
    g_eff = jnp.minimum(g, item_end[-1] - 1)
    tile = _count_below(item_end[None, :], g_eff, True)
    k = g_eff - (item_end[tile] - n_items_tile[tile])
    items = [tile, live.astype(i32), (live & (k == 0)).astype(i32),
             (live & (k == n_items_tile[tile] - 1)).astype(i32), expert[tile * n_parts]]
    for part in range(n_parts):
        sub = tile * n_parts + part
        lo = first_blk[sub] + k * n_win
        hi = jnp.where(span[sub] > 0, jnp.minimum(lo + n_win - 1, last_blk[sub]), lo - 1)
        items += [jnp.clip(lo, 0, max(nb - n_win, 0)), lo, hi, r0[sub]]

    per = tb_combine // tb
    cblk = cend.reshape(n_tokens // tb_combine, per, n_experts)[:, -1, :]
    a = jnp.concatenate([jnp.zeros((1, n_experts), i32), cblk[:-1]], axis=0) + goff[None, :]
    b = cblk + goff[None, :]
    win_blk = a // tr_gather
    win_cnt = jnp.where(b > a, (b - 1) // tr_gather - win_blk + 1, 0)
    maxb = jnp.maximum(total // tr_gather - 1, 0).reshape(1)
    return dict(n_rows=n_rows, items=tuple(items), win_blk=win_blk.reshape(-1),
                win_cnt=win_cnt.reshape(-1), goff=goff, maxb=maxb)


def _pad_cols(w, width):
    return jnp.pad(w, ((0, 0), (0, width - w.shape[1])))


def kernel(x, attn_norm, attn_w_in, attn_b_forget, attn_w_out, ffn_norm, ffn_w_gate, ffn_w_up,
           ffn_w_down, conv_norm, conv_w_in, conv_w, conv_w_out, moe_norm, moe_w_router,
           moe_w_gate, moe_w_up, moe_w_down, final_norm):
    b, s, d = x.shape
    n_heads = attn_b_forget.shape[-1]
    n_experts = moe_w_router.shape[-1]
    dh = d // n_heads
    n = b * s
    tm = min(512, s)

    w_in = attn_w_in[0]
    wqk = w_in[:, :2 * d].astype(bf16)
    wvt = w_in[:, 2 * d:3 * d].T.astype(bf16)
    wf = _pad_cols(w_in[:, 3 * d:], LANES).astype(bf16)
    bfg = _pad_cols(attn_b_forget[0][None, :], LANES)
    q, k, vt, c, ct = _attn_in(x, attn_norm[0][None, :], wqk, wvt, wf, bfg, n_heads=n_heads, tm=tm)
    o = _attention(q, k, vt, c, ct, n_heads=n_heads, tq=min(512, s))
    h2 = _attn_out_ffn(x.reshape(n, d), o.reshape(n, d), attn_w_out[0].astype(bf16),
                       ffn_norm[0][None, :], ffn_w_gate[0].astype(bf16), ffn_w_up[0].astype(bf16),
                       ffn_w_down[0].astype(bf16), tm=tm)

    wr = _pad_cols(moe_w_router[0], LANES)
    wrh = wr.astype(bf16)
    wrl = (wr - wrh.astype(f32)).astype(bf16)
    tb = min(MOE_TOKEN_BLOCK, tm)
    h3, hn3, _, rank, gate_t, rank_t, cend = _conv_route(
        h2.reshape(b, s, d), conv_norm[0][None, :], conv_w_in[0].astype(bf16), conv_w[0],
        conv_w_out[0].astype(bf16), moe_norm[0][None, :], wrh, wrl, n_experts=n_experts, tm=tm, tb=tb)
    tb_combine = min(MOE_COMBINE_BLOCK, n)
    plan = _moe_plan(cend.reshape(n // tb, LANES)[:, :n_experts].astype(jnp.int32), n_tokens=n,
                     n_experts=n_experts, tb=tb, tr_gather=MOE_GATHER_ROWS, n_parts=MOE_FFN_PARTS,
                     tb_combine=tb_combine, n_win=MOE_GATHER_WINDOW)
    ys = _moe_expert(plan["items"], hn3.reshape(n, d), rank_t, gate_t, moe_w_gate[0].astype(bf16),
                     moe_w_up[0].astype(bf16), moe_w_down[0].astype(bf16), n_rows=plan["n_rows"],
                     n_parts=MOE_FFN_PARTS, n_win=MOE_GATHER_WINDOW, tg=MOE_GATHER_ROWS, tb=tb)
    n_win_combine = (tb_combine - 1 + MOE_GATHER_ROWS - 1) // MOE_GATHER_ROWS + 1
    out = _moe_combine(plan["win_blk"], plan["win_cnt"], plan["goff"], plan["maxb"], h3.reshape(n, d),
                       rank.reshape(n, LANES), ys, final_norm[None, :], n_experts=n_experts,
                       n_win=n_win_combine, tr=MOE_GATHER_ROWS, tb=tb_combine)
    return out.reshape(b, s, d)
```

```python
import functools
import math

import jax
import jax.numpy as jnp
from jax import lax
from jax.experimental import pallas as pl
from jax.experimental.pallas import tpu as pltpu

RMS_EPS = 1e-6
LANES = 128
NEG_BIG = -1e30
LOG2E = 1.4426950408889634
BF16_ROWS = 16
SUBLANES = 8
NO_RANK = -float(2 ** 22)
MOE_TOKEN_BLOCK = 256
MOE_GATHER_ROWS = 256
MOE_GATHER_WINDOW = 6
MOE_FFN_PARTS = 2
MOE_COMBINE_BLOCK = 512
VMEM_LIMIT = 56 * 1024 * 1024

f32 = jnp.float32
bf16 = jnp.bfloat16


def _rmsnorm(x, g):
    return x * lax.rsqrt(jnp.mean(x * x, axis=-1, keepdims=True) + RMS_EPS) * g


def _cumsum_rows(x):
    n = x.shape[0]
    row = lax.broadcasted_iota(jnp.int32, x.shape, 0)
    d = 1
    while d < n:
        x = x + jnp.where(row >= d, pltpu.roll(x, d, 0), 0.0)
        d *= 2
    return x


def _const_spec(shape):
    return pl.BlockSpec(shape, lambda *_: (0,) * len(shape), pipeline_mode=pl.Buffered(1))


def _attn_in_kernel(x_ref, g_ref, wk_ref, wqvt_ref, wf_ref, bf_ref,
                    qt_ref, k_ref, vt_ref, c_ref, ct_ref, carry_ref, *, n_heads):
    @pl.when(pl.program_id(1) == 0)
    def _():
        carry_ref[...] = jnp.zeros_like(carry_ref)

    x = x_ref[0]
    d = x.shape[1]
    hn = _rmsnorm(x, g_ref[...]).astype(bf16)
    k_ref[0] = jnp.dot(hn, wk_ref[...], preferred_element_type=f32).astype(bf16)
    qvt = lax.dot_general(wqvt_ref[...], hn, (((1,), (1,)), ((), ())), preferred_element_type=f32)
    qt_ref[0] = (qvt[:d] * (LOG2E / math.sqrt(d // n_heads))).astype(bf16)
    vt_ref[0] = qvt[d:].astype(bf16)
    fl = jnp.dot(hn, wf_ref[...], preferred_element_type=f32) + bf_ref[...]
    logf = jnp.minimum(fl, 0.0) - jnp.log1p(jnp.exp(-jnp.abs(fl)))
    c = _cumsum_rows(logf * LOG2E) + carry_ref[...]
    carry_ref[...] = c[c.shape[0] - 1:, :]
    c_ref[0] = c
    ct_ref[0] = c.T[:n_heads, :]


def _attn_in(x, g, wk, wqvt, wf, bfg, *, n_heads, tm):
    b, s, d = x.shape
    grid = (b, s // tm)
    return pl.pallas_call(
        functools.partial(_attn_in_kernel, n_heads=n_heads),
        grid=grid,
        in_specs=[
            pl.BlockSpec((1, tm, d), lambda i, j: (i, j, 0)),
            _const_spec((1, d)),
            _const_spec((d, d)),
            _const_spec((2 * d, d)),
            _const_spec((d, LANES)),
            _const_spec((1, LANES)),
        ],
        out_specs=[
            pl.BlockSpec((1, d, tm), lambda i, j: (i, 0, j)),
            pl.BlockSpec((1, tm, d), lambda i, j: (i, j, 0)),
            pl.BlockSpec((1, d, tm), lambda i, j: (i, 0, j)),
            pl.BlockSpec((1, tm, LANES), lambda i, j: (i, j, 0)),
            pl.BlockSpec((1, n_heads, tm), lambda i, j: (i, 0, j)),
        ],
        out_shape=[
            jax.ShapeDtypeStruct((b, d, s), bf16),
            jax.ShapeDtypeStruct((b, s, d), bf16),
            jax.ShapeDtypeStruct((b, d, s), bf16),
            jax.ShapeDtypeStruct((b, s, LANES), f32),
            jax.ShapeDtypeStruct((b, n_heads, s), f32),
        ],
        scratch_shapes=[pltpu.VMEM((1, LANES), f32)],
        compiler_params=pltpu.CompilerParams(
            dimension_semantics=("parallel", "arbitrary"), vmem_limit_bytes=VMEM_LIMIT),
        name="attn_in",
    )(x, g, wk, wqvt, wf, bfg)


def _attn_kernel(qt_ref, k_ref, vt_ref, c_ref, ct_ref, o_ref, kaug_ref, qcat_ref, m_ref, acc_ref,
                 t0_ref, mx0_ref, t1_ref, mx1_ref, p_ref, *, tq, dh, seq):
    hp = pl.program_id(1)
    lane = lax.broadcasted_iota(jnp.int32, (1, LANES), 1)
    rows = 256

    def build(i, carry):
        r0 = pl.multiple_of(i * rows, rows)
        cb = c_ref[0, pl.ds(r0, rows), :]
        aug = jnp.zeros((rows, LANES), f32)
        for h in range(2):
            rest = -jnp.sum(jnp.where(lane == 2 * hp + h, cb, 0.0), axis=1, keepdims=True)
            for piece in range(3):
                part = rest.astype(bf16).astype(f32)
                aug = jnp.where(lane == 3 * h + piece, part, aug)
                rest = rest - part
        kaug_ref[pl.ds(r0, rows), :] = aug.astype(bf16)
        return carry

    lax.fori_loop(0, seq // rows, build, 0)

    feat = lax.broadcasted_iota(jnp.int32, (LANES, 1), 0)
    head_rows = (feat < dh, feat >= dh)
    ones_rows = jnp.ones((BF16_ROWS, tq), bf16)
    qw = 2 * LANES
    kc = 64

    def q_block(qb, carry0):
        q0 = pl.multiple_of(qb * tq, tq)
        qblk = qt_ref[0, :, pl.ds(q0, tq)]
        for h in range(2):
            qm = jnp.where(head_rows[h], qblk, jnp.zeros_like(qblk))
            pick = jnp.where((feat >= 3 * h) & (feat < 3 * h + 3), 1.0, 0.0).astype(bf16)
            qcat_ref[h] = jnp.concatenate([qm, jnp.broadcast_to(pick, (LANES, tq))], axis=0)
            m_ref[h] = jnp.full((1, tq), NEG_BIG, f32)
            acc_ref[h] = jnp.zeros((dh + BF16_ROWS, tq), f32)

        def scores_into(j, t_ref, mx_ref, diagonal):
            k0 = pl.multiple_of(j * tq, tq)
            kcat = jnp.concatenate([k_ref[0, pl.ds(k0, tq), :], kaug_ref[pl.ds(k0, tq), :]], axis=1)
            for h in range(2):
                for c0 in range(0, tq, qw):
                    st = jnp.dot(kcat, qcat_ref[h, :, c0:c0 + qw], preferred_element_type=f32)
                    if diagonal:
                        kpos = lax.broadcasted_iota(jnp.int32, (tq, qw), 0)
                        qpos = lax.broadcasted_iota(jnp.int32, (tq, qw), 1) + c0
                        st = jnp.where(kpos <= qpos, st, NEG_BIG)
                    t_ref[h, :, c0:c0 + qw] = st
                    mx_ref[h, :, c0:c0 + qw] = jnp.max(st, axis=0, keepdims=True)

        def softmax_pv(j, t_ref, mx_ref):
            k0 = pl.multiple_of(j * tq, tq)
            for h in range(2):
                cq = ct_ref[0, 0, h:h + 1, pl.ds(q0, tq)]
                m = m_ref[h]
                m_new = jnp.maximum(m, mx_ref[h] + cq)
                m_ref[h] = m_new
                shift = cq - m_new
                for r0 in range(0, tq, kc):
                    p_ref[h, r0:r0 + kc, :] = jnp.exp2(t_ref[h, r0:r0 + kc, :] + shift).astype(bf16)
                vta = jnp.concatenate([vt_ref[0, pl.ds(h * dh, dh), pl.ds(k0, tq)], ones_rows], axis=0)
                pv = jnp.dot(vta, p_ref[h], preferred_element_type=f32)
                acc_ref[h] = jnp.exp2(m - m_new) * acc_ref[h] + pv

        def block_at(n):
            return jnp.where(n == 0, qb, n - 1)

        scores_into(qb, t0_ref, mx0_ref, True)

        def pair(i, carry):
            n = 2 * i
            scores_into(block_at(n + 1), t1_ref, mx1_ref, False)
            softmax_pv(block_at(n), t0_ref, mx0_ref)
            scores_into(block_at(n + 2), t0_ref, mx0_ref, False)
            softmax_pv(block_at(n + 1), t1_ref, mx1_ref)
            return carry

        lax.fori_loop(0, qb // 2, pair, 0)

        @pl.when(qb % 2 == 1)
        def _():
            scores_into(qb - 1, t1_ref, mx1_ref, False)
            softmax_pv(block_at(qb - 1), t0_ref, mx0_ref)
            softmax_pv(qb - 1, t1_ref, mx1_ref)

        @pl.when(qb % 2 == 0)
        def _():
            softmax_pv(block_at(qb), t0_ref, mx0_ref)

        ot = jnp.concatenate([acc_ref[h, :dh, :] / acc_ref[h, dh:dh + 1, :] for h in range(2)], axis=0)
        o_ref[0, pl.ds(q0, tq), :] = ot.T.astype(bf16)
        return carry0

    lax.fori_loop(0, seq // tq, q_block, 0)


def _attention(qt, k, vt, c, ct, *, n_heads, tq):
    b, s, d = k.shape
    dh = d // n_heads
    assert 2 * dh == LANES, "kernel packs exactly two heads per 128-lane block"
    n_pairs = n_heads // 2
    ct4 = ct.reshape(b, n_pairs, 2, s)
    return pl.pallas_call(
        functools.partial(_attn_kernel, tq=tq, dh=dh, seq=s),
        grid=(b, n_pairs),
        in_specs=[
            pl.BlockSpec((1, LANES, s), lambda i, j: (i, j, 0)),
            pl.BlockSpec((1, s, LANES), lambda i, j: (i, 0, j)),
            pl.BlockSpec((1, LANES, s), lambda i, j: (i, j, 0)),
            pl.BlockSpec((1, s, LANES), lambda i, j: (i, 0, 0)),
            pl.BlockSpec((1, 1, 2, s), lambda i, j: (i, j, 0, 0)),
        ],
        out_specs=pl.BlockSpec((1, s, LANES), lambda i, j: (i, 0, j)),
        out_shape=jax.ShapeDtypeStruct((b, s, d), bf16),
        scratch_shapes=[
            pltpu.VMEM((s, LANES), bf16),
            pltpu.VMEM((2, 2 * LANES, tq), bf16),
            pltpu.VMEM((2, 1, tq), f32),
            pltpu.VMEM((2, dh + BF16_ROWS, tq), f32),
            pltpu.VMEM((2, tq, tq), f32), pltpu.VMEM((2, 1, tq), f32),
            pltpu.VMEM((2, tq, tq), f32), pltpu.VMEM((2, 1, tq), f32),
            pltpu.VMEM((2, tq, tq), bf16),
        ],
        compiler_params=pltpu.CompilerParams(
            dimension_semantics=("parallel", "arbitrary"), vmem_limit_bytes=VMEM_LIMIT),
        name="fox_attention",
    )(qt, k, vt, c, ct4)


def _ff_chunks(f):
    step = 1024 if f % 256 == 0 and f > 1024 else f
    return [(c0, min(c0 + step, f)) for c0 in range(0, f, step)]


def _swiglu(hn, wg_ref, wu_ref, wd_ref):
    y = None
    for c0, c1 in _ff_chunks(wg_ref.shape[-1]):
        g = jnp.dot(hn, wg_ref[:, c0:c1], preferred_element_type=f32)
        u = jnp.dot(hn, wu_ref[:, c0:c1], preferred_element_type=f32)
        a = (g * (1.0 / (1.0 + jnp.exp(-g))) * u).astype(bf16)
        part = jnp.dot(a, wd_ref[c0:c1, :], preferred_element_type=f32)
        y = part if y is None else y + part
    return y


def _attn_out_ffn_kernel(x_ref, o_ref, wo_ref, g_ref, wg_ref, wu_ref, wd_ref, out_ref):
    h1 = x_ref[...] + jnp.dot(o_ref[...], wo_ref[...], preferred_element_type=f32)
    hn = _rmsnorm(h1, g_ref[...]).astype(bf16)
    out_ref[...] = h1 + _swiglu(hn, wg_ref, wu_ref, wd_ref)


def _attn_out_ffn(x, o, wo, g, wg, wu, wd, *, tm):
    n, d = x.shape
    f = wg.shape[1]
    return pl.pallas_call(
        _attn_out_ffn_kernel,
        grid=(n // tm,),
        in_specs=[
            pl.BlockSpec((tm, d), lambda i: (i, 0)),
            pl.BlockSpec((tm, d), lambda i: (i, 0)),
            _const_spec((d, d)),
            _const_spec((1, d)),
            _const_spec((d, f)),
            _const_spec((d, f)),
            _const_spec((f, d)),
        ],
        out_specs=pl.BlockSpec((tm, d), lambda i: (i, 0)),
        out_shape=jax.ShapeDtypeStruct((n, d), f32),
        compiler_params=pltpu.CompilerParams(
            dimension_semantics=("parallel",), vmem_limit_bytes=VMEM_LIMIT),
        name="attn_out_ffn",
    )(x, o, wo, g, wg, wu, wd)


def _conv_route_kernel(h_ref, g_ref, win_ref, cw_ref, wout_ref, g2_ref, wrh_ref, wrl_ref,
                       h3_ref, hn3_ref, gate_ref, rank_ref, gate_t_ref, rank_t_ref, cend_ref,
                       tail_ref, cnt_ref, *, n_experts, width, tb):
    @pl.when(pl.program_id(1) == 0)
    def _():
        tail_ref[...] = jnp.zeros_like(tail_ref)

    @pl.when((pl.program_id(0) == 0) & (pl.program_id(1) == 0))
    def _():
        cnt_ref[...] = jnp.zeros_like(cnt_ref)

    h = h_ref[0]
    tm, d = h.shape
    hn = _rmsnorm(h, g_ref[...]).astype(bf16)
    proj = jnp.dot(hn, win_ref[...], preferred_element_type=f32)
    gate_b, gate_c, xv = proj[:, :d], proj[:, d:2 * d], proj[:, 2 * d:]
    u = gate_c * xv
    tail = tail_ref[...]
    row8 = lax.broadcasted_iota(jnp.int32, tail.shape, 0)
    conv = cw_ref[width - 1:width, :] * u
    for back in range(1, width):
        shifted = pltpu.roll(u, back, 0)
        head = jnp.where(row8 < back, pltpu.roll(tail, back, 0), shifted[:8])
        shifted = jnp.concatenate([head, shifted[8:]], axis=0)
        conv = conv + cw_ref[width - 1 - back:width - back, :] * shifted
    tail_ref[...] = u[tm - 8:, :]
    y = jnp.dot((gate_b * conv).astype(bf16), wout_ref[...], preferred_element_type=f32)
    h3 = h + y
    h3_ref[0] = h3

    hn3 = _rmsnorm(h3, g2_ref[...])
    hi = hn3.astype(bf16)
    lo = (hn3 - hi.astype(f32)).astype(bf16)
    hn3_ref[0] = hi
    logits = (jnp.dot(hi, wrh_ref[...], preferred_element_type=f32)
              + jnp.dot(lo, wrh_ref[...], preferred_element_type=f32)
              + jnp.dot(hi, wrl_ref[...], preferred_element_type=f32))
    lane = lax.broadcasted_iota(jnp.int32, logits.shape, 1)
    logits = jnp.where(lane < n_experts, logits, -jnp.inf)
    top1 = jnp.max(logits, axis=1, keepdims=True)
    idx1 = jnp.min(jnp.where(logits == top1, lane, LANES), axis=1, keepdims=True)
    rest = jnp.where(lane == idx1, -jnp.inf, logits)
    top2 = jnp.max(rest, axis=1, keepdims=True)
    idx2 = jnp.min(jnp.where(rest == top2, lane, LANES), axis=1, keepdims=True)
    e2 = jnp.exp(top2 - top1)
    w1 = 1.0 / (1.0 + e2)
    w2 = e2 / (1.0 + e2)
    gate = jnp.where(lane == idx1, w1, 0.0) + jnp.where(lane == idx2, w2, 0.0)
    gate_ref[0] = gate
    gate_t_ref[...] = gate.T[:SUBLANES, :]

    sel = jnp.where((lane == idx1) | (lane == idx2), 1.0, 0.0)
    incl = _cumsum_rows(sel) + cnt_ref[...]
    rank = jnp.where(sel > 0.0, incl - 1.0, NO_RANK)
    rank_ref[0] = rank
    rank_t_ref[...] = rank.T[:SUBLANES, :]
    for i in range(tm // tb):
        cend_ref[0, i:i + 1, :] = incl[(i + 1) * tb - 1:(i + 1) * tb, :]
    cnt_ref[...] = incl[tm - 1:, :]


def _conv_route(h, g, win, cw, wout, g2, wrh, wrl, *, n_experts, tm, tb):
    b, s, d = h.shape
    width = cw.shape[0]
    assert n_experts <= SUBLANES
    nb = s // tm
    return pl.pallas_call(
        functools.partial(_conv_route_kernel, n_experts=n_experts, width=width, tb=tb),
        grid=(b, nb),
        in_specs=[
            pl.BlockSpec((1, tm, d), lambda i, j: (i, j, 0)),
            _const_spec((1, d)),
            _const_spec((d, 3 * d)),
            _const_spec((width, d)),
            _const_spec((d, d)),
            _const_spec((1, d)),
            _const_spec((d, LANES)),
            _const_spec((d, LANES)),
        ],
        out_specs=[
            pl.BlockSpec((1, tm, d), lambda i, j: (i, j, 0)),
            pl.BlockSpec((1, tm, d), lambda i, j: (i, j, 0)),
            pl.BlockSpec((1, tm, LANES), lambda i, j: (i, j, 0)),
            pl.BlockSpec((1, tm, LANES), lambda i, j: (i, j, 0)),
            pl.BlockSpec((SUBLANES, tm), lambda i, j: (0, i * nb + j)),
            pl.BlockSpec((SUBLANES, tm), lambda i, j: (0, i * nb + j)),
            pl.BlockSpec((1, tm // tb, LANES), lambda i, j: (i * nb + j, 0, 0)),
        ],
        out_shape=[
            jax.ShapeDtypeStruct((b, s, d), f32),
            jax.ShapeDtypeStruct((b, s, d), bf16),
            jax.ShapeDtypeStruct((b, s, LANES), f32),
            jax.ShapeDtypeStruct((b, s, LANES), f32),
            jax.ShapeDtypeStruct((SUBLANES, b * s), f32),
            jax.ShapeDtypeStruct((SUBLANES, b * s), f32),
            jax.ShapeDtypeStruct((b * nb, tm // tb, LANES), f32),
        ],
        scratch_shapes=[pltpu.VMEM((8, d), f32), pltpu.VMEM((1, LANES), f32)],
        compiler_params=pltpu.CompilerParams(
            dimension_semantics=("arbitrary", "arbitrary"), vmem_limit_bytes=VMEM_LIMIT),
        name="conv_route",
    )(h, g, win, cw, wout, g2, wrh, wrl)


N_ITEM_FIELDS = 5
N_PART_FIELDS = 4


def _moe_expert_kernel(*refs, n_parts, n_win, tg, tb):
    n_scalar = N_ITEM_FIELDS + N_PART_FIELDS * n_parts
    tile_ref, live_ref, first_ref, last_ref, exp_ref = refs[:N_ITEM_FIELDS]
    part_refs = refs[N_ITEM_FIELDS:n_scalar]
    n_blk = n_parts * n_win
    hn_refs = refs[n_scalar:n_scalar + n_blk]
    rank_refs = refs[n_scalar + n_blk:n_scalar + 2 * n_blk]
    gate_refs = refs[n_scalar + 2 * n_blk:n_scalar + 3 * n_blk]
    wg_ref, wu_ref, wd_ref, ys_ref, acc_ref, gacc_ref = refs[n_scalar + 3 * n_blk:]
    g = pl.program_id(0)

    @pl.when(first_ref[g] == 1)
    def _():
        acc_ref[...] = jnp.zeros_like(acc_ref)
        gacc_ref[...] = jnp.zeros_like(gacc_ref)

    @pl.when(live_ref[g] == 1)
    def _():
        e = exp_ref[g]
        row = lax.broadcasted_iota(jnp.int32, (tg, tb), 0).astype(f32)
        for part in range(n_parts):
            blk_ref, lo_ref, hi_ref, r0_ref = part_refs[N_PART_FIELDS * part:N_PART_FIELDS * (part + 1)]
            rows_sum, gate_sum = None, None
            for k in range(n_win):
                blk = blk_ref[g] + k
                r0 = jnp.where((blk >= lo_ref[g]) & (blk <= hi_ref[g]), r0_ref[g], 2 ** 20).astype(f32)
                src = part * n_win + k
                hit = rank_refs[src][pl.ds(e, 1), :] - r0 == row
                onehot = jnp.where(hit, 1.0, 0.0).astype(bf16)
                rows = jnp.dot(onehot, hn_refs[src][...], preferred_element_type=f32)
                gates = jnp.sum(jnp.where(hit, gate_refs[src][pl.ds(e, 1), :], 0.0), axis=1, keepdims=True)
                rows_sum = rows if rows_sum is None else rows_sum + rows
                gate_sum = gates if gate_sum is None else gate_sum + gates
            acc_ref[part * tg:(part + 1) * tg, :] += rows_sum
            gacc_ref[part * tg:(part + 1) * tg, :] += gate_sum

    @pl.when(last_ref[g] == 1)
    def _():
        y = _swiglu(acc_ref[...].astype(bf16), wg_ref.at[0], wu_ref.at[0], wd_ref.at[0])
        ys_ref[...] = (y * gacc_ref[:, 0:1]).astype(bf16)


def _moe_expert(items, hn3, rank_t, gate_t, wg, wu, wd, *, n_rows, n_parts, n_win, tg, tb):
    n, d = hn3.shape
    f = wg.shape[2]
    nb = n // tb
    tr = n_parts * tg
    n_items = items[0].shape[0]

    def tok_map(part, k, rows_first):
        def index_map(g, *scalars):
            blk = scalars[N_ITEM_FIELDS + N_PART_FIELDS * part]
            b = jnp.minimum(blk[g] + k, nb - 1)
            return (b, 0) if rows_first else (0, b)
        return index_map

    def weight_spec(shape):
        return pl.BlockSpec(shape, lambda g, *scalars: (scalars[4][g], 0, 0), pipeline_mode=pl.Buffered(1))

    windows = [(part, k) for part in range(n_parts) for k in range(n_win)]
    in_specs = ([pl.BlockSpec((tb, d), tok_map(part, k, True)) for part, k in windows]
                + [pl.BlockSpec((SUBLANES, tb), tok_map(part, k, False)) for part, k in windows]
                + [pl.BlockSpec((SUBLANES, tb), tok_map(part, k, False)) for part, k in windows]
                + [weight_spec((1, d, f)), weight_spec((1, d, f)), weight_spec((1, f, d))])
    n_blk = len(windows)
    return pl.pallas_call(
        functools.partial(_moe_expert_kernel, n_parts=n_parts, n_win=n_win, tg=tg, tb=tb),
        grid_spec=pltpu.PrefetchScalarGridSpec(
            num_scalar_prefetch=len(items),
            grid=(n_items,),
            in_specs=in_specs,
            out_specs=pl.BlockSpec((tr, d), lambda g, *scalars: (scalars[0][g], 0)),
            scratch_shapes=[pltpu.VMEM((tr, d), f32), pltpu.VMEM((tr, LANES), f32)],
        ),
        out_shape=jax.ShapeDtypeStruct((n_rows, d), bf16),
        compiler_params=pltpu.CompilerParams(
            dimension_semantics=("arbitrary",), vmem_limit_bytes=VMEM_LIMIT),
        name="moe_expert",
    )(*items, *([hn3] * n_blk), *([rank_t] * n_blk), *([gate_t] * n_blk), wg, wu, wd)


def _moe_combine_kernel(wb_ref, nb_ref, goff_ref, maxb_ref, h3_ref, rank_ref, *refs,
                        n_experts, n_win, tr):
    ys_refs = refs[:n_experts * n_win]
    gf_ref, out_ref, acc_ref = refs[n_experts * n_win:]
    i = pl.program_id(0)
    col = lax.broadcasted_iota(jnp.int32, (1, tr), 1).astype(f32)

    def picked(e, k):
        row = rank_ref[:, e:e + 1] + goff_ref[e].astype(f32)
        base = ((wb_ref[i * n_experts + e] + k) * tr).astype(f32)
        onehot = jnp.where(row - base == col, 1.0, 0.0).astype(bf16)
        return jnp.dot(onehot, ys_refs[e * n_win + k][...], preferred_element_type=f32)

    total = h3_ref[...]
    for e in range(n_experts):
        total = total + picked(e, 0)
    acc_ref[...] = total
    for e in range(n_experts):
        for k in range(1, n_win):
            @pl.when(k < nb_ref[i * n_experts + e])
            def _(e=e, k=k):
                acc_ref[...] += picked(e, k)
    out_ref[...] = _rmsnorm(acc_ref[...], gf_ref[...])


def _moe_combine(win_blk, win_cnt, goff, maxb, h3, rank, ys, gf, *, n_experts, n_win, tr, tb):
    n, d = h3.shape

    def ys_map(e, k):
        def index_map(i, wb, nb, go, mb):
            return (jnp.minimum(wb[i * n_experts + e] + k, mb[0]), 0)
        return index_map

    tok = lambda i, *_: (i, 0)
    return pl.pallas_call(
        functools.partial(_moe_combine_kernel, n_experts=n_experts, n_win=n_win, tr=tr),
        grid_spec=pltpu.PrefetchScalarGridSpec(
            num_scalar_prefetch=4,
            grid=(n // tb,),
            in_specs=([pl.BlockSpec((tb, d), tok), pl.BlockSpec((tb, LANES), tok)]
                      + [pl.BlockSpec((tr, d), ys_map(e, k))
                         for e in range(n_experts) for k in range(n_win)]
                      + [pl.BlockSpec((1, d), lambda i, *_: (0, 0))]),
            out_specs=pl.BlockSpec((tb, d), tok),
            scratch_shapes=[pltpu.VMEM((tb, d), f32)],
        ),
        out_shape=jax.ShapeDtypeStruct((n, d), f32),
        compiler_params=pltpu.CompilerParams(
            dimension_semantics=("arbitrary",), vmem_limit_bytes=VMEM_LIMIT),
        name="moe_combine",
    )(win_blk, win_cnt, goff, maxb, h3, rank, *([ys] * (n_experts * n_win)), gf)


def _count_below(table, values, inclusive):
    cmp = table <= values[:, None] if inclusive else table < values[:, None]
    return jnp.sum(cmp, axis=1).astype(jnp.int32)


def _moe_plan(cend, *, n_tokens, n_experts, tb, tr_gather, n_parts, tb_combine, n_win):
    i32 = jnp.int32
    nb = n_tokens // tb
    tr_ffn = n_parts * tr_gather
    counts = cend[-1]
    gsz = (counts + tr_ffn - 1) // tr_ffn * tr_ffn
    gend = jnp.cumsum(gsz).astype(i32)
    goff = gend - gsz
    total = gend[-1]
    n_rows = 2 * n_tokens + n_experts * tr_ffn

    start = jnp.arange(n_rows // tr_gather, dtype=i32) * tr_gather
    expert = jnp.minimum(_count_below(gend[None, :], start, True), n_experts - 1)
    r0 = start - goff[expert]
    r1 = jnp.minimum(r0 + tr_gather, counts[expert])
    table = cend.T[expert]
    first_blk = _count_below(table, r0, True)
    last_blk = _count_below(table, r1, False)
    span = jnp.where((start < total) & (r1 > r0), last_blk - first_blk + 1, 0)

    n_tiles = n_rows // tr_ffn
    tile_valid = jnp.arange(n_tiles, dtype=i32) * tr_ffn < total
    need = jnp.max(((span + n_win - 1) // n_win).reshape(n_tiles, n_parts), axis=1)
    n_items_tile = jnp.where(tile_valid, jnp.maximum(need, 1), 0)
    item_end = jnp.cumsum(n_items_tile).astype(i32)
    n_items_max = n_tiles + n_experts * ((nb + n_win - 1) // n_win)
    g = jnp.arange(n_items_max, dtype=i32)
    live = g < item_end[-1]
    g_eff = jnp.minimum(g, item_end[-1] - 1)
    tile = _count_below(item_end[None, :], g_eff, True)
    k = g_eff - (item_end[tile] - n_items_tile[tile])
    items = [tile, live.astype(i32), (live & (k == 0)).astype(i32),
             (live & (k == n_items_tile[tile] - 1)).astype(i32), expert[tile * n_parts]]
    for part in range(n_parts):
        sub = tile * n_parts + part
        lo = first_blk[sub] + k * n_win
        hi = jnp.where(span[sub] > 0, jnp.minimum(lo + n_win - 1, last_blk[sub]), lo - 1)
        items += [jnp.clip(lo, 0, max(nb - n_win, 0)), lo, hi, r0[sub]]

    per = tb_combine // tb
    cblk = cend.reshape(n_tokens // tb_combine, per, n_experts)[:, -1, :]
    a = jnp.concatenate([jnp.zeros((1, n_experts), i32), cblk[:-1]], axis=0) + goff[None, :]
    b = cblk + goff[None, :]
    win_blk = a // tr_gather
    win_cnt = jnp.where(b > a, (b - 1) // tr_gather - win_blk + 1, 0)
    maxb = jnp.maximum(total // tr_gather - 1, 0).reshape(1)
    return dict(n_rows=n_rows, items=tuple(items), win_blk=win_blk.reshape(-1),
                win_cnt=win_cnt.reshape(-1), goff=goff, maxb=maxb)


def _pad_cols(w, width):
    return jnp.pad(w, ((0, 0), (0, width - w.shape[1])))


def kernel(x, attn_norm, attn_w_in, attn_b_forget, attn_w_out, ffn_norm, ffn_w_gate, ffn_w_up,
           ffn_w_down, conv_norm, conv_w_in, conv_w, conv_w_out, moe_norm, moe_w_router,
           moe_w_gate, moe_w_up, moe_w_down, final_norm):
    b, s, d = x.shape
    n_heads = attn_b_forget.shape[-1]
    n_experts = moe_w_router.shape[-1]
    dh = d // n_heads
    n = b * s
    tm = min(512, s)

    w_in = attn_w_in[0]
    wk = w_in[:, d:2 * d].astype(bf16)
    wqvt = jnp.concatenate([w_in[:, :d], w_in[:, 2 * d:3 * d]], axis=1).T.astype(bf16)
    wf = _pad_cols(w_in[:, 3 * d:], LANES).astype(bf16)
    bfg = _pad_cols(attn_b_forget[0][None, :], LANES)
    qt, k, vt, c, ct = _attn_in(x, attn_norm[0][None, :], wk, wqvt, wf, bfg, n_heads=n_heads, tm=tm)
    o = _attention(qt, k, vt, c, ct, n_heads=n_heads, tq=min(512, s))
    h2 = _attn_out_ffn(x.reshape(n, d), o.reshape(n, d), attn_w_out[0].astype(bf16),
                       ffn_norm[0][None, :], ffn_w_gate[0].astype(bf16), ffn_w_up[0].astype(bf16),
                       ffn_w_down[0].astype(bf16), tm=tm)

    wr = _pad_cols(moe_w_router[0], LANES)
    wrh = wr.astype(bf16)
    wrl = (wr - wrh.astype(f32)).astype(bf16)
    tb = min(MOE_TOKEN_BLOCK, tm)
    h3, hn3, _, rank, gate_t, rank_t, cend = _conv_route(
        h2.reshape(b, s, d), conv_norm[0][None, :], conv_w_in[0].astype(bf16), conv_w[0],
        conv_w_out[0].astype(bf16), moe_norm[0][None, :], wrh, wrl, n_experts=n_experts, tm=tm, tb=tb)
    tb_combine = min(MOE_COMBINE_BLOCK, n)
    plan = _moe_plan(cend.reshape(n // tb, LANES)[:, :n_experts].astype(jnp.int32), n_tokens=n,
                     n_experts=n_experts, tb=tb, tr_gather=MOE_GATHER_ROWS, n_parts=MOE_FFN_PARTS,
                     tb_combine=tb_combine, n_win=MOE_GATHER_WINDOW)
    ys = _moe_expert(plan["items"], hn3.reshape(n, d), rank_t, gate_t, moe_w_gate[0].astype(bf16),
                     moe_w_up[0].astype(bf16), moe_w_down[0].astype(bf16), n_rows=plan["n_rows"],
                     n_parts=MOE_FFN_PARTS, n_win=MOE_GATHER_WINDOW, tg=MOE_GATHER_ROWS, tb=tb)
    n_win_combine = (tb_combine - 1 + MOE_GATHER_ROWS - 1) // MOE_GATHER_ROWS + 1
    out = _moe_combine(plan["win_blk"], plan["win_cnt"], plan["goff"], plan["maxb"], h3.reshape(n, d),
                       rank.reshape(n, LANES), ys, final_norm[None, :], n_experts=n_experts,
                       n_win=n_win_combine, tr=MOE_GATHER_ROWS, tb=tb_combine)
    return out.reshape(b, s, d)
```

```python
import functools
import math

import jax
import jax.numpy as jnp
from jax import lax
from jax.experimental import pallas as pl
from jax.experimental.pallas import tpu as pltpu

RMS_EPS = 1e-6
LANES = 128
NEG_BIG = -1e30
LOG2E = 1.4426950408889634
BF16_ROWS = 16
SUBLANES = 8
NO_RANK = -float(2 ** 22)
MOE_TOKEN_BLOCK = 256
MOE_GATHER_ROWS = 256
MOE_GATHER_WINDOW = 6
MOE_FFN_PARTS = 2
MOE_COMBINE_BLOCK = 512
VMEM_LIMIT = 56 * 1024 * 1024

f32 = jnp.float32
bf16 = jnp.bfloat16


def _rmsnorm(x, g):
    return x * lax.rsqrt(jnp.mean(x * x, axis=-1, keepdims=True) + RMS_EPS) * g


def _cumsum_rows(x):
    n = x.shape[0]
    row = lax.broadcasted_iota(jnp.int32, x.shape, 0)
    d = 1
    while d < n:
        x = x + jnp.where(row >= d, pltpu.roll(x, d, 0), 0.0)
        d *= 2
    return x


def _const_spec(shape):
    return pl.BlockSpec(shape, lambda *_: (0,) * len(shape), pipeline_mode=pl.Buffered(1))


def _attn_in_kernel(x_ref, g_ref, wk_ref, wqvt_ref, wf_ref, bf_ref,
                    qt_ref, k_ref, vt_ref, c_ref, ct_ref, carry_ref, *, n_heads):
    @pl.when(pl.program_id(1) == 0)
    def _():
        carry_ref[...] = jnp.zeros_like(carry_ref)

    x = x_ref[0]
    d = x.shape[1]
    hn = _rmsnorm(x, g_ref[...]).astype(bf16)
    k_ref[0] = jnp.dot(hn, wk_ref[...], preferred_element_type=f32).astype(bf16)
    qvt = lax.dot_general(wqvt_ref[...], hn, (((1,), (1,)), ((), ())), preferred_element_type=f32)
    qt_ref[0] = (qvt[:d] * (LOG2E / math.sqrt(d // n_heads))).astype(bf16)
    vt_ref[0] = qvt[d:].astype(bf16)
    fl = jnp.dot(hn, wf_ref[...], preferred_element_type=f32) + bf_ref[...]
    logf = jnp.minimum(fl, 0.0) - jnp.log1p(jnp.exp(-jnp.abs(fl)))
    c = _cumsum_rows(logf * LOG2E) + carry_ref[...]
    carry_ref[...] = c[c.shape[0] - 1:, :]
    c_ref[0] = c
    ct_ref[0] = c.T[:n_heads, :]


def _attn_in(x, g, wk, wqvt, wf, bfg, *, n_heads, tm):
    b, s, d = x.shape
    grid = (b, s // tm)
    return pl.pallas_call(
        functools.partial(_attn_in_kernel, n_heads=n_heads),
        grid=grid,
        in_specs=[
            pl.BlockSpec((1, tm, d), lambda i, j: (i, j, 0)),
            _const_spec((1, d)),
            _const_spec((d, d)),
            _const_spec((2 * d, d)),
            _const_spec((d, LANES)),
            _const_spec((1, LANES)),
        ],
        out_specs=[
            pl.BlockSpec((1, d, tm), lambda i, j: (i, 0, j)),
            pl.BlockSpec((1, tm, d), lambda i, j: (i, j, 0)),
            pl.BlockSpec((1, d, tm), lambda i, j: (i, 0, j)),
            pl.BlockSpec((1, tm, LANES), lambda i, j: (i, j, 0)),
            pl.BlockSpec((1, n_heads, tm), lambda i, j: (i, 0, j)),
        ],
        out_shape=[
            jax.ShapeDtypeStruct((b, d, s), bf16),
            jax.ShapeDtypeStruct((b, s, d), bf16),
            jax.ShapeDtypeStruct((b, d, s), bf16),
            jax.ShapeDtypeStruct((b, s, LANES), f32),
            jax.ShapeDtypeStruct((b, n_heads, s), f32),
        ],
        scratch_shapes=[pltpu.VMEM((1, LANES), f32)],
        compiler_params=pltpu.CompilerParams(
            dimension_semantics=("parallel", "arbitrary"), vmem_limit_bytes=VMEM_LIMIT),
        name="attn_in",
    )(x, g, wk, wqvt, wf, bfg)


def _attn_kernel(qt_ref, k_ref, vt_ref, c_ref, ct_ref, o_ref, kaug_ref, qcat_ref, m_ref, acc_ref,
                 t0_ref, mx0_ref, t1_ref, mx1_ref, p_ref, *, tq, dh, seq):
    hp = pl.program_id(1)
    lane = lax.broadcasted_iota(jnp.int32, (1, LANES), 1)
    rows = 256

    def build(i, carry):
        r0 = pl.multiple_of(i * rows, rows)
        cb = c_ref[0, pl.ds(r0, rows), :]
        aug = jnp.zeros((rows, LANES), f32)
        for h in range(2):
            rest = -jnp.sum(jnp.where(lane == 2 * hp + h, cb, 0.0), axis=1, keepdims=True)
            for piece in range(3):
                part = rest.astype(bf16).astype(f32)
                aug = jnp.where(lane == 3 * h + piece, part, aug)
                rest = rest - part
        kaug_ref[pl.ds(r0, rows), :] = aug.astype(bf16)
        return carry

    lax.fori_loop(0, seq // rows, build, 0)

    feat = lax.broadcasted_iota(jnp.int32, (LANES, 1), 0)
    head_rows = (feat < dh, feat >= dh)
    ones_rows = jnp.ones((BF16_ROWS, tq), bf16)
    qw = 2 * LANES
    kc = 64

    def q_block(qb, carry0):
        q0 = pl.multiple_of(qb * tq, tq)
        qblk = qt_ref[0, :, pl.ds(q0, tq)]
        for h in range(2):
            qm = jnp.where(head_rows[h], qblk, jnp.zeros_like(qblk))
            pick = jnp.where((feat >= 3 * h) & (feat < 3 * h + 3), 1.0, 0.0).astype(bf16)
            qcat_ref[h] = jnp.concatenate([qm, jnp.broadcast_to(pick, (LANES, tq))], axis=0)
            m_ref[h] = jnp.full((1, tq), NEG_BIG, f32)
            acc_ref[h] = jnp.zeros((dh + BF16_ROWS, tq), f32)

        def scores_into(j, t_ref, mx_ref, diagonal):
            k0 = pl.multiple_of(j * tq, tq)
            kcat = jnp.concatenate([k_ref[0, pl.ds(k0, tq), :], kaug_ref[pl.ds(k0, tq), :]], axis=1)
            for h in range(2):
                for c0 in range(0, tq, qw):
                    st = jnp.dot(kcat, qcat_ref[h, :, c0:c0 + qw], preferred_element_type=f32)
                    if diagonal:
                        kpos = lax.broadcasted_iota(jnp.int32, (tq, qw), 0)
                        qpos = lax.broadcasted_iota(jnp.int32, (tq, qw), 1) + c0
                        st = jnp.where(kpos <= qpos, st, NEG_BIG)
                    t_ref[h, :, c0:c0 + qw] = st
                    mx_ref[h, :, c0:c0 + qw] = jnp.max(st, axis=0, keepdims=True)

        def softmax_pv(j, t_ref, mx_ref):
            k0 = pl.multiple_of(j * tq, tq)
            for h in range(2):
                cq = ct_ref[0, 0, h:h + 1, pl.ds(q0, tq)]
                m = m_ref[h]
                m_new = jnp.maximum(m, mx_ref[h] + cq)
                m_ref[h] = m_new
                shift = cq - m_new
                for r0 in range(0, tq, kc):
                    p_ref[h, r0:r0 + kc, :] = jnp.exp2(t_ref[h, r0:r0 + kc, :] + shift).astype(bf16)
                vta = jnp.concatenate([vt_ref[0, pl.ds(h * dh, dh), pl.ds(k0, tq)], ones_rows], axis=0)
                pv = jnp.dot(vta, p_ref[h], preferred_element_type=f32)
                acc_ref[h] = jnp.exp2(m - m_new) * acc_ref[h] + pv

        def block_at(n):
            return jnp.where(n == 0, qb, n - 1)

        scores_into(qb, t0_ref, mx0_ref, True)

        def pair(i, carry):
            n = 2 * i
            scores_into(block_at(n + 1), t1_ref, mx1_ref, False)
            softmax_pv(block_at(n), t0_ref, mx0_ref)
            scores_into(block_at(n + 2), t0_ref, mx0_ref, False)
            softmax_pv(block_at(n + 1), t1_ref, mx1_ref)
            return carry

        lax.fori_loop(0, qb // 2, pair, 0)

        @pl.when(qb % 2 == 1)
        def _():
            scores_into(qb - 1, t1_ref, mx1_ref, False)
            softmax_pv(block_at(qb - 1), t0_ref, mx0_ref)
            softmax_pv(qb - 1, t1_ref, mx1_ref)

        @pl.when(qb % 2 == 0)
        def _():
            softmax_pv(block_at(qb), t0_ref, mx0_ref)

        ot = jnp.concatenate([acc_ref[h, :dh, :] / acc_ref[h, dh:dh + 1, :] for h in range(2)], axis=0)
        o_ref[0, pl.ds(q0, tq), :] = ot.T.astype(bf16)
        return carry0

    lax.fori_loop(0, seq // tq, q_block, 0)


def _attention(qt, k, vt, c, ct, *, n_heads, tq):
    b, s, d = k.shape
    dh = d // n_heads
    assert 2 * dh == LANES, "kernel packs exactly two heads per 128-lane block"
    n_pairs = n_heads // 2
    ct4 = ct.reshape(b, n_pairs, 2, s)
    return pl.pallas_call(
        functools.partial(_attn_kernel, tq=tq, dh=dh, seq=s),
        grid=(b, n_pairs),
        in_specs=[
            pl.BlockSpec((1, LANES, s), lambda i, j: (i, j, 0)),
            pl.BlockSpec((1, s, LANES), lambda i, j: (i, 0, j)),
            pl.BlockSpec((1, LANES, s), lambda i, j: (i, j, 0)),
            pl.BlockSpec((1, s, LANES), lambda i, j: (i, 0, 0)),
            pl.BlockSpec((1, 1, 2, s), lambda i, j: (i, j, 0, 0)),
        ],
        out_specs=pl.BlockSpec((1, s, LANES), lambda i, j: (i, 0, j)),
        out_shape=jax.ShapeDtypeStruct((b, s, d), bf16),
        scratch_shapes=[
            pltpu.VMEM((s, LANES), bf16),
            pltpu.VMEM((2, 2 * LANES, tq), bf16),
            pltpu.VMEM((2, 1, tq), f32),
            pltpu.VMEM((2, dh + BF16_ROWS, tq), f32),
            pltpu.VMEM((2, tq, tq), f32), pltpu.VMEM((2, 1, tq), f32),
            pltpu.VMEM((2, tq, tq), f32), pltpu.VMEM((2, 1, tq), f32),
            pltpu.VMEM((2, tq, tq), bf16),
        ],
        compiler_params=pltpu.CompilerParams(
            dimension_semantics=("parallel", "arbitrary"), vmem_limit_bytes=VMEM_LIMIT),
        name="fox_attention",
    )(qt, k, vt, c, ct4)


def _ff_chunks(f):
    step = 1024 if f % 256 == 0 and f > 1024 else f
    return [(c0, min(c0 + step, f)) for c0 in range(0, f, step)]


def _swiglu(hn, wg_ref, wu_ref, wd_ref):
    y = None
    for c0, c1 in _ff_chunks(wg_ref.shape[-1]):
        g = jnp.dot(hn, wg_ref[:, c0:c1], preferred_element_type=f32)
        u = jnp.dot(hn, wu_ref[:, c0:c1], preferred_element_type=f32)
        a = (g * (1.0 / (1.0 + jnp.exp(-g))) * u).astype(bf16)
        part = jnp.dot(a, wd_ref[c0:c1, :], preferred_element_type=f32)
        y = part if y is None else y + part
    return y


def _attn_out_ffn_kernel(x_ref, o_ref, wo_ref, g_ref, wg_ref, wu_ref, wd_ref, out_ref):
    h1 = x_ref[...] + jnp.dot(o_ref[...], wo_ref[...], preferred_element_type=f32)
    hn = _rmsnorm(h1, g_ref[...]).astype(bf16)
    out_ref[...] = h1 + _swiglu(hn, wg_ref, wu_ref, wd_ref)


def _attn_out_ffn(x, o, wo, g, wg, wu, wd, *, tm):
    n, d = x.shape
    f = wg.shape[1]
    return pl.pallas_call(
        _attn_out_ffn_kernel,
        grid=(n // tm,),
        in_specs=[
            pl.BlockSpec((tm, d), lambda i: (i, 0)),
            pl.BlockSpec((tm, d), lambda i: (i, 0)),
            _const_spec((d, d)),
            _const_spec((1, d)),
            _const_spec((d, f)),
            _const_spec((d, f)),
            _const_spec((f, d)),
        ],
        out_specs=pl.BlockSpec((tm, d), lambda i: (i, 0)),
        out_shape=jax.ShapeDtypeStruct((n, d), f32),
        compiler_params=pltpu.CompilerParams(
            dimension_semantics=("parallel",), vmem_limit_bytes=VMEM_LIMIT),
        name="attn_out_ffn",
    )(x, o, wo, g, wg, wu, wd)


def _conv_route_kernel(h_ref, g_ref, win_ref, cw_ref, wout_ref, g2_ref, wrh_ref, wrl_ref,
                       h3_ref, hn3_ref, rank_ref, rg_t_ref, cend_ref,
                       tail_ref, cnt_ref, *, n_experts, width, tb):
    @pl.when(pl.program_id(1) == 0)
    def _():
        tail_ref[...] = jnp.zeros_like(tail_ref)

    @pl.when((pl.program_id(0) == 0) & (pl.program_id(1) == 0))
    def _():
        cnt_ref[...] = jnp.zeros_like(cnt_ref)

    h = h_ref[0]
    tm, d = h.shape
    hn = _rmsnorm(h, g_ref[...]).astype(bf16)
    proj = jnp.dot(hn, win_ref[...], preferred_element_type=f32)
    gate_b, gate_c, xv = proj[:, :d], proj[:, d:2 * d], proj[:, 2 * d:]
    u = gate_c * xv
    tail = tail_ref[...]
    row8 = lax.broadcasted_iota(jnp.int32, tail.shape, 0)
    conv = cw_ref[width - 1:width, :] * u
    for back in range(1, width):
        shifted = pltpu.roll(u, back, 0)
        head = jnp.where(row8 < back, pltpu.roll(tail, back, 0), shifted[:8])
        shifted = jnp.concatenate([head, shifted[8:]], axis=0)
        conv = conv + cw_ref[width - 1 - back:width - back, :] * shifted
    tail_ref[...] = u[tm - 8:, :]
    y = jnp.dot((gate_b * conv).astype(bf16), wout_ref[...], preferred_element_type=f32)
    h3 = h + y
    h3_ref[0] = h3

    hn3 = _rmsnorm(h3, g2_ref[...])
    hi = hn3.astype(bf16)
    lo = (hn3 - hi.astype(f32)).astype(bf16)
    hn3_ref[0] = hi
    logits = (jnp.dot(hi, wrh_ref[...], preferred_element_type=f32)
              + jnp.dot(lo, wrh_ref[...], preferred_element_type=f32)
              + jnp.dot(hi, wrl_ref[...], preferred_element_type=f32))
    lane = lax.broadcasted_iota(jnp.int32, logits.shape, 1)
    logits = jnp.where(lane < n_experts, logits, -jnp.inf)
    top1 = jnp.max(logits, axis=1, keepdims=True)
    idx1 = jnp.min(jnp.where(logits == top1, lane, LANES), axis=1, keepdims=True)
    rest = jnp.where(lane == idx1, -jnp.inf, logits)
    top2 = jnp.max(rest, axis=1, keepdims=True)
    idx2 = jnp.min(jnp.where(rest == top2, lane, LANES), axis=1, keepdims=True)
    e2 = jnp.exp(top2 - top1)
    w1 = 1.0 / (1.0 + e2)
    w2 = e2 / (1.0 + e2)
    gate = jnp.where(lane == idx1, w1, 0.0) + jnp.where(lane == idx2, w2, 0.0)

    sel = jnp.where((lane == idx1) | (lane == idx2), 1.0, 0.0)
    incl = _cumsum_rows(sel) + cnt_ref[...]
    rank = jnp.where(sel > 0.0, incl - 1.0, NO_RANK)
    rank_ref[0] = rank
    rg_t_ref[...] = jnp.concatenate([rank.T[:SUBLANES, :], gate.T[:SUBLANES, :]], axis=0)
    for i in range(tm // tb):
        cend_ref[0, i:i + 1, :] = incl[(i + 1) * tb - 1:(i + 1) * tb, :]
    cnt_ref[...] = incl[tm - 1:, :]


def _conv_route(h, g, win, cw, wout, g2, wrh, wrl, *, n_experts, tm, tb):
    b, s, d = h.shape
    width = cw.shape[0]
    assert n_experts <= SUBLANES
    nb = s // tm
    return pl.pallas_call(
        functools.partial(_conv_route_kernel, n_experts=n_experts, width=width, tb=tb),
        grid=(b, nb),
        in_specs=[
            pl.BlockSpec((1, tm, d), lambda i, j: (i, j, 0)),
            _const_spec((1, d)),
            _const_spec((d, 3 * d)),
            _const_spec((width, d)),
            _const_spec((d, d)),
            _const_spec((1, d)),
            _const_spec((d, LANES)),
            _const_spec((d, LANES)),
        ],
        out_specs=[
            pl.BlockSpec((1, tm, d), lambda i, j: (i, j, 0)),
            pl.BlockSpec((1, tm, d), lambda i, j: (i, j, 0)),
            pl.BlockSpec((1, tm, LANES), lambda i, j: (i, j, 0)),
            pl.BlockSpec((2 * SUBLANES, tm), lambda i, j: (0, i * nb + j)),
            pl.BlockSpec((1, tm // tb, LANES), lambda i, j: (i * nb + j, 0, 0)),
        ],
        out_shape=[
            jax.ShapeDtypeStruct((b, s, d), f32),
            jax.ShapeDtypeStruct((b, s, d), bf16),
            jax.ShapeDtypeStruct((b, s, LANES), f32),
            jax.ShapeDtypeStruct((2 * SUBLANES, b * s), f32),
            jax.ShapeDtypeStruct((b * nb, tm // tb, LANES), f32),
        ],
        scratch_shapes=[pltpu.VMEM((8, d), f32), pltpu.VMEM((1, LANES), f32)],
        compiler_params=pltpu.CompilerParams(
            dimension_semantics=("arbitrary", "arbitrary"), vmem_limit_bytes=VMEM_LIMIT),
        name="conv_route",
    )(h, g, win, cw, wout, g2, wrh, wrl)


N_ITEM_FIELDS = 5
N_PART_FIELDS = 4


def _moe_expert_kernel(*refs, n_parts, n_win, tg, tb):
    n_scalar = N_ITEM_FIELDS + N_PART_FIELDS * n_parts
    tile_ref, live_ref, first_ref, last_ref, exp_ref = refs[:N_ITEM_FIELDS]
    part_refs = refs[N_ITEM_FIELDS:n_scalar]
    n_blk = n_parts * n_win
    hn_refs = refs[n_scalar:n_scalar + n_blk]
    rg_refs = refs[n_scalar + n_blk:n_scalar + 2 * n_blk]
    wg_ref, wu_ref, wd_ref, ys_ref, acc_ref, gacc_ref = refs[n_scalar + 2 * n_blk:]
    g = pl.program_id(0)

    @pl.when(first_ref[g] == 1)
    def _():
        acc_ref[...] = jnp.zeros_like(acc_ref)
        gacc_ref[...] = jnp.zeros_like(gacc_ref)

    @pl.when(live_ref[g] == 1)
    def _():
        e = exp_ref[g]
        row = lax.broadcasted_iota(jnp.int32, (tg, tb), 0).astype(f32)
        for part in range(n_parts):
            blk_ref, lo_ref, hi_ref, r0_ref = part_refs[N_PART_FIELDS * part:N_PART_FIELDS * (part + 1)]
            rows_sum, gate_sum = None, None
            for k in range(n_win):
                blk = blk_ref[g] + k
                r0 = jnp.where((blk >= lo_ref[g]) & (blk <= hi_ref[g]), r0_ref[g], 2 ** 20).astype(f32)
                src = part * n_win + k
                hit = rg_refs[src][pl.ds(e, 1), :] - r0 == row
                onehot = jnp.where(hit, 1.0, 0.0).astype(bf16)
                rows = jnp.dot(onehot, hn_refs[src][...], preferred_element_type=f32)
                gates = jnp.sum(jnp.where(hit, rg_refs[src][pl.ds(SUBLANES + e, 1), :], 0.0),
                                axis=1, keepdims=True)
                rows_sum = rows if rows_sum is None else rows_sum + rows
                gate_sum = gates if gate_sum is None else gate_sum + gates
            acc_ref[part * tg:(part + 1) * tg, :] += rows_sum
            gacc_ref[part * tg:(part + 1) * tg, :] += gate_sum

    @pl.when(last_ref[g] == 1)
    def _():
        y = _swiglu(acc_ref[...].astype(bf16), wg_ref.at[0], wu_ref.at[0], wd_ref.at[0])
        ys_ref[...] = (y * gacc_ref[:, 0:1]).astype(bf16)


def _moe_expert(items, hn3, rg_t, wg, wu, wd, *, n_rows, n_parts, n_win, tg, tb):
    n, d = hn3.shape
    f = wg.shape[2]
    nb = n // tb
    tr = n_parts * tg
    n_items = items[0].shape[0]

    def tok_map(part, k, rows_first):
        def index_map(g, *scalars):
            blk = scalars[N_ITEM_FIELDS + N_PART_FIELDS * part]
            b = jnp.minimum(blk[g] + k, nb - 1)
            return (b, 0) if rows_first else (0, b)
        return index_map

    def weight_spec(shape):
        return pl.BlockSpec(shape, lambda g, *scalars: (scalars[4][g], 0, 0), pipeline_mode=pl.Buffered(1))

    windows = [(part, k) for part in range(n_parts) for k in range(n_win)]
    in_specs = ([pl.BlockSpec((tb, d), tok_map(part, k, True)) for part, k in windows]
                + [pl.BlockSpec((2 * SUBLANES, tb), tok_map(part, k, False)) for part, k in windows]
                + [weight_spec((1, d, f)), weight_spec((1, d, f)), weight_spec((1, f, d))])
    n_blk = len(windows)
    return pl.pallas_call(
        functools.partial(_moe_expert_kernel, n_parts=n_parts, n_win=n_win, tg=tg, tb=tb),
        grid_spec=pltpu.PrefetchScalarGridSpec(
            num_scalar_prefetch=len(items),
            grid=(n_items,),
            in_specs=in_specs,
            out_specs=pl.BlockSpec((tr, d), lambda g, *scalars: (scalars[0][g], 0)),
            scratch_shapes=[pltpu.VMEM((tr, d), f32), pltpu.VMEM((tr, LANES), f32)],
        ),
        out_shape=jax.ShapeDtypeStruct((n_rows, d), bf16),
        compiler_params=pltpu.CompilerParams(
            dimension_semantics=("arbitrary",), vmem_limit_bytes=VMEM_LIMIT),
        name="moe_expert",
    )(*items, *([hn3] * n_blk), *([rg_t] * n_blk), wg, wu, wd)


def _moe_combine_kernel(wb_ref, nb_ref, goff_ref, maxb_ref, h3_ref, rank_ref, *refs,
                        n_experts, n_win, tr):
    ys_refs = refs[:n_experts * n_win]
    gf_ref, out_ref, acc_ref = refs[n_experts * n_win:]
    i = pl.program_id(0)
    col = lax.broadcasted_iota(jnp.int32, (1, tr), 1).astype(f32)

    def picked(e, k):
        row = rank_ref[:, e:e + 1] + goff_ref[e].astype(f32)
        base = ((wb_ref[i * n_experts + e] + k) * tr).astype(f32)
        onehot = jnp.where(row - base == col, 1.0, 0.0).astype(bf16)
        return jnp.dot(onehot, ys_refs[e * n_win + k][...], preferred_element_type=f32)

    total = h3_ref[...]
    for e in range(n_experts):
        total = total + picked(e, 0)
    acc_ref[...] = total
    for e in range(n_experts):
        for k in range(1, n_win):
            @pl.when(k < nb_ref[i * n_experts + e])
            def _(e=e, k=k):
                acc_ref[...] += picked(e, k)
    out_ref[...] = _rmsnorm(acc_ref[...], gf_ref[...])


def _moe_combine(win_blk, win_cnt, goff, maxb, h3, rank, ys, gf, *, n_experts, n_win, tr, tb):
    n, d = h3.shape

    def ys_map(e, k):
        def index_map(i, wb, nb, go, mb):
            return (jnp.minimum(wb[i * n_experts + e] + k, mb[0]), 0)
        return index_map

    tok = lambda i, *_: (i, 0)
    return pl.pallas_call(
        functools.partial(_moe_combine_kernel, n_experts=n_experts, n_win=n_win, tr=tr),
        grid_spec=pltpu.PrefetchScalarGridSpec(
            num_scalar_prefetch=4,
            grid=(n // tb,),
            in_specs=([pl.BlockSpec((tb, d), tok), pl.BlockSpec((tb, LANES), tok)]
                      + [pl.BlockSpec((tr, d), ys_map(e, k))
                         for e in range(n_experts) for k in range(n_win)]
                      + [pl.BlockSpec((1, d), lambda i, *_: (0, 0))]),
            out_specs=pl.BlockSpec((tb, d), tok),
            scratch_shapes=[pltpu.VMEM((tb, d), f32)],
        ),
        out_shape=jax.ShapeDtypeStruct((n, d), f32),
        compiler_params=pltpu.CompilerParams(
            dimension_semantics=("arbitrary",), vmem_limit_bytes=VMEM_LIMIT),
        name="moe_combine",
    )(win_blk, win_cnt, goff, maxb, h3, rank, *([ys] * (n_experts * n_win)), gf)


def _count_below(table, values, inclusive):
    cmp = table <= values[:, None] if inclusive else table < values[:, None]
    return jnp.sum(cmp, axis=1).astype(jnp.int32)


def _moe_plan(cend, *, n_tokens, n_experts, tb, tr_gather, n_parts, tb_combine, n_win):
    i32 = jnp.int32
    nb = n_tokens // tb
    tr_ffn = n_parts * tr_gather
    counts = cend[-1]
    gsz = (counts + tr_ffn - 1) // tr_ffn * tr_ffn
    gend = jnp.cumsum(gsz).astype(i32)
    goff = gend - gsz
    total = gend[-1]
    n_rows = 2 * n_tokens + n_experts * tr_ffn

    start = jnp.arange(n_rows // tr_gather, dtype=i32) * tr_gather
    expert = jnp.minimum(_count_below(gend[None, :], start, True), n_experts - 1)
    r0 = start - goff[expert]
    r1 = jnp.minimum(r0 + tr_gather, counts[expert])
    table = cend.T[expert]
    first_blk = _count_below(table, r0, True)
    last_blk = _count_below(table, r1, False)
    span = jnp.where((start < total) & (r1 > r0), last_blk - first_blk + 1, 0)

    n_tiles = n_rows // tr_ffn
    tile_valid = jnp.arange(n_tiles, dtype=i32) * tr_ffn < total
    need = jnp.max(((span + n_win - 1) // n_win).reshape(n_tiles, n_parts), axis=1)
    n_items_tile = jnp.where(tile_valid, jnp.maximum(need, 1), 0)
    item_end = jnp.cumsum(n_items_tile).astype(i32)
    n_items_max = n_tiles + n_experts * ((nb + n_win - 1) // n_win)
    g = jnp.arange(n_items_max, dtype=i32)
    live = g < item_end[-1]
    g_eff = jnp.minimum(g, item_end[-1] - 1)
    tile = _count_below(item_end[None, :], g_eff, True)
    k = g_eff - (item_end[tile] - n_items_tile[tile])
    items = [tile, live.astype(i32), (live & (k == 0)).astype(i32),
             (live & (k == n_items_tile[tile] - 1)).astype(i32), expert[tile * n_parts]]
    for part in range(n_parts):
        sub = tile * n_parts + part
        lo = first_blk[sub] + k * n_win
        hi = jnp.where(span[sub] > 0, jnp.minimum(lo + n_win - 1, last_blk[sub]), lo - 1)
        items += [jnp.clip(lo, 0, max(nb - n_win, 0)), lo, hi, r0[sub]]

    per = tb_combine // tb
    cblk = cend.reshape(n_tokens // tb_combine, per, n_experts)[:, -1, :]
    a = jnp.concatenate([jnp.zeros((1, n_experts), i32), cblk[:-1]], axis=0) + goff[None, :]
    b = cblk + goff[None, :]
    win_blk = a // tr_gather
    win_cnt = jnp.where(b > a, (b - 1) // tr_gather - win_blk + 1, 0)
    maxb = jnp.maximum(total // tr_gather - 1, 0).reshape(1)
    return dict(n_rows=n_rows, items=tuple(items), win_blk=win_blk.reshape(-1),
                win_cnt=win_cnt.reshape(-1), goff=goff, maxb=maxb)


def _pad_cols(w, width):
    return jnp.pad(w, ((0, 0), (0, width - w.shape[1])))


def kernel(x, attn_norm, attn_w_in, attn_b_forget, attn_w_out, ffn_norm, ffn_w_gate, ffn_w_up,
           ffn_w_down, conv_norm, conv_w_in, conv_w, conv_w_out, moe_norm, moe_w_router,
           moe_w_gate, moe_w_up, moe_w_down, final_norm):
    b, s, d = x.shape
    n_heads = attn_b_forget.shape[-1]
    n_experts = moe_w_router.shape[-1]
    dh = d // n_heads
    n = b * s
    tm = min(512, s)

    w_in = attn_w_in[0]
    wk = w_in[:, d:2 * d].astype(bf16)
    wqvt = jnp.concatenate([w_in[:, :d], w_in[:, 2 * d:3 * d]], axis=1).T.astype(bf16)
    wf = _pad_cols(w_in[:, 3 * d:], LANES).astype(bf16)
    bfg = _pad_cols(attn_b_forget[0][None, :], LANES)
    qt, k, vt, c, ct = _attn_in(x, attn_norm[0][None, :], wk, wqvt, wf, bfg, n_heads=n_heads, tm=tm)
    o = _attention(qt, k, vt, c, ct, n_heads=n_heads, tq=min(512, s))
    h2 = _attn_out_ffn(x.reshape(n, d), o.reshape(n, d), attn_w_out[0].astype(bf16),
                       ffn_norm[0][None, :], ffn_w_gate[0].astype(bf16), ffn_w_up[0].astype(bf16),
                       ffn_w_down[0].astype(bf16), tm=tm)

    wr = _pad_cols(moe_w_router[0], LANES)
    wrh = wr.astype(bf16)
    wrl = (wr - wrh.astype(f32)).astype(bf16)
    tb = min(MOE_TOKEN_BLOCK, tm)
    h3, hn3, rank, rg_t, cend = _conv_route(
        h2.reshape(b, s, d), conv_norm[0][None, :], conv_w_in[0].astype(bf16), conv_w[0],
        conv_w_out[0].astype(bf16), moe_norm[0][None, :], wrh, wrl, n_experts=n_experts, tm=tm, tb=tb)
    tb_combine = min(MOE_COMBINE_BLOCK, n)
    plan = _moe_plan(cend.reshape(n // tb, LANES)[:, :n_experts].astype(jnp.int32), n_tokens=n,
                     n_experts=n_experts, tb=tb, tr_gather=MOE_GATHER_ROWS, n_parts=MOE_FFN_PARTS,
                     tb_combine=tb_combine, n_win=MOE_GATHER_WINDOW)
    ys = _moe_expert(plan["items"], hn3.reshape(n, d), rg_t, moe_w_gate[0].astype(bf16),
                     moe_w_up[0].astype(bf16), moe_w_down[0].astype(bf16), n_rows=plan["n_rows"],
                     n_parts=MOE_FFN_PARTS, n_win=MOE_GATHER_WINDOW, tg=MOE_GATHER_ROWS, tb=tb)
    n_win_combine = (tb_combine - 1 + MOE_GATHER_ROWS - 1) // MOE_GATHER_ROWS + 1
    out = _moe_combine(plan["win_blk"], plan["win_cnt"], plan["goff"], plan["maxb"], h3.reshape(n, d),
                       rank.reshape(n, LANES), ys, final_norm[None, :], n_experts=n_experts,
                       n_win=n_win_combine, tr=MOE_GATHER_ROWS, tb=tb_combine)
    return out.reshape(b, s, d)
```
